```python
import math
import jax
import jax.numpy as jnp
from jax import lax
import numpy as np

D_MODEL = 2048
BATCH = 4
SEQ = 2048
DEPTH = 2
DEC_BATCH = 8
DEC_SEQ = 4096
PAST_LEN = 128

N_EVEN = (DEPTH + 1) // 2
N_ODD = DEPTH // 2
NORM_EPS = 1e-6
L2_EPS = 1e-6

RET_HEADS = 8
RET_DK = 64
RET_DV = 128
RET_CHUNK = 128
ROPE_BASE = 10000.0
RET_QK_W = RET_HEADS * RET_DK
RET_V_W = RET_HEADS * RET_DV
GM_GROUPS = 8
GM_GROUP_DIM = 128
GM_CHUNK = 128
GM_W = GM_GROUPS * GM_GROUP_DIM
EVEN_SPLITS = [RET_QK_W, 2 * RET_QK_W, 2 * RET_QK_W + RET_V_W, 2 * RET_QK_W + 2 * RET_V_W,
               2 * RET_QK_W + 2 * RET_V_W + GM_W]
EVEN_IN_W = 2 * RET_QK_W + 2 * RET_V_W + 2 * GM_W
EVEN_MIX_W = RET_V_W + GM_W
GDN_K_HEADS = 16
GDN_V_HEADS = 32
GDN_DK = 128
GDN_DV = 128
GDN_CHUNK = 64
GDN_CONV = 5
GDN_QK_W = GDN_K_HEADS * GDN_DK
GDN_V_W = GDN_V_HEADS * GDN_DV
GDN_CONV_W = 2 * GDN_QK_W + GDN_V_W
ODD_SPLITS = [GDN_CONV_W, GDN_CONV_W + GDN_V_W, GDN_CONV_W + GDN_V_W + GDN_V_HEADS,
              GDN_CONV_W + GDN_V_W + 2 * GDN_V_HEADS, GDN_CONV_W + GDN_V_W + 3 * GDN_V_HEADS]
ODD_IN_W = GDN_CONV_W + GDN_V_W + 4 * GDN_V_HEADS
D_FF = -(-8 * D_MODEL // (3 * 256)) * 256

kernel_name = 'hybrid_bidir_retention_gmlp_gdn_encoder'


def rms_norm(x, w):
    xf = x.astype(jnp.float32)
    y = xf * lax.rsqrt(jnp.mean(xf * xf, axis=-1, keepdims=True) + NORM_EPS)
    return (y * w.astype(jnp.float32)).astype(x.dtype)


def standardize(x):
    xf = x.astype(jnp.float32)
    xc = xf - jnp.mean(xf, axis=-1, keepdims=True)
    return xc * lax.rsqrt(jnp.mean(xc * xc, axis=-1, keepdims=True) + NORM_EPS)


def layer_norm(x, w, b):
    return (standardize(x) * w.astype(jnp.float32) + b.astype(jnp.float32)).astype(x.dtype)


def l2_normalize(x):
    xf = x.astype(jnp.float32)
    return xf * lax.rsqrt(jnp.sum(xf * xf, axis=-1, keepdims=True) + L2_EPS)


def apply_rotary(x, pos):
    half = x.shape[-1] // 2
    inv = ROPE_BASE ** (-jnp.arange(half, dtype=jnp.float32) / half)
    ang = pos.astype(jnp.float32)[:, None] * inv[None, :]
    cos = jnp.cos(ang)[None, :, None, :]
    sin = jnp.sin(ang)[None, :, None, :]
    xf = x.astype(jnp.float32)
    x1, x2 = xf[..., :half], xf[..., half:]
    return jnp.concatenate([x1 * cos - x2 * sin, x1 * sin + x2 * cos], axis=-1)


def exclusive_chunk_scan(u, decay, reverse):
    def step(s, u_n):
        return s * decay[:, None, None] + u_n, s
    _, states = lax.scan(step, jnp.zeros_like(u[0]), u, reverse=reverse)
    return states


def bidir_retention(q, k, v, log_gamma):
    B, S, H, dk = q.shape
    dv = v.shape[-1]
    C = RET_CHUNK
    N = S // C
    q = q.reshape(B, N, C, H, dk)
    k = k.reshape(B, N, C, H, dk)
    v = v.reshape(B, N, C, H, dv)
    lf, lb = log_gamma[0], log_gamma[1]
    idx = jnp.arange(C, dtype=jnp.float32)
    diff = idx[:, None] - idx[None, :]
    adiff = jnp.abs(diff)[None]
    dmat = jnp.where((diff >= 0)[None], jnp.exp(adiff * lf[:, None, None]),
                     jnp.exp(adiff * lb[:, None, None]))
    scores = jnp.einsum('bnihd,bnjhd->bnhij', q, k) * dmat
    out = jnp.einsum('bnhij,bnjhe->bnihe', scores, v)
    kf = k * jnp.exp((C - 1 - idx)[:, None] * lf[None, :])[None, None, :, :, None]
    s_f = exclusive_chunk_scan(jnp.einsum('bnjhd,bnjhe->nbhde', kf, v), jnp.exp(C * lf), False)
    qf = q * jnp.exp((idx + 1)[:, None] * lf[None, :])[None, None, :, :, None]
    out = out + jnp.einsum('bnihd,nbhde->bnihe', qf, s_f)
    kb = k * jnp.exp((idx + 1)[:, None] * lb[None, :])[None, None, :, :, None]
    s_b = exclusive_chunk_scan(jnp.einsum('bnjhd,bnjhe->nbhde', kb, v), jnp.exp(C * lb), True)
    qb = q * jnp.exp((C - 1 - idx)[:, None] * lb[None, :])[None, None, :, :, None]
    out = out + jnp.einsum('bnihd,nbhde->bnihe', qb, s_b)
    return out.reshape(B, S, H, dv)


def retention_gmlp_mixer(h, w_in, w_out, ret_decay_logit, ret_gn_w, gm_ln_w, gm_ln_b, gm_ws, gm_bs):
    B, S, _ = h.shape
    proj = h @ w_in
    q, k, v, g, gu, gv = jnp.split(proj, EVEN_SPLITS, axis=-1)
    pos = jnp.arange(S)
    q = apply_rotary(q.reshape(B, S, RET_HEADS, RET_DK), pos) * (RET_DK ** -0.5)
    k = apply_rotary(k.reshape(B, S, RET_HEADS, RET_DK), pos)
    v = v.reshape(B, S, RET_HEADS, RET_DV).astype(jnp.float32)
    log_gamma = jax.nn.log_sigmoid(ret_decay_logit.astype(jnp.float32))
    y = bidir_retention(q, k, v, log_gamma)
    y = standardize(y).reshape(B, S, RET_V_W) * ret_gn_w.astype(jnp.float32)
    y_a = (jax.nn.silu(g.astype(jnp.float32)) * y).astype(h.dtype)
    gu = jax.nn.gelu(gu)
    gv = layer_norm(jax.nn.gelu(gv), gm_ln_w, gm_ln_b)
    gv = gv.reshape(B, S // GM_CHUNK, GM_CHUNK, GM_GROUPS, GM_GROUP_DIM)
    mixed = jnp.einsum('gij,bnjgc->bnigc', gm_ws, gv) + gm_bs.T[None, None, :, :, None]
    y_b = gu * mixed.reshape(B, S, GM_W)
    return jnp.concatenate([y_a, y_b], axis=-1) @ w_out


def gdn_chunk(q, k, v, g, beta):
    B, S, H, dk = q.shape
    dv = v.shape[-1]
    C = GDN_CHUNK
    N = S // C

    def chunks(x):
        return x.astype(jnp.float32).reshape(B, N, C, H, -1).transpose(0, 3, 1, 2, 4)

    q, k, v = chunks(q), chunks(k), chunks(v)
    g = g.astype(jnp.float32).reshape(B, N, C, H).transpose(0, 3, 1, 2)
    beta = beta.astype(jnp.float32).reshape(B, N, C, H).transpose(0, 3, 1, 2)
    gc = jnp.cumsum(g, axis=-1)
    lower = jnp.tril(jnp.ones((C, C), dtype=bool))
    strict = jnp.tril(jnp.ones((C, C), dtype=bool), -1)
    diff = gc[..., :, None] - gc[..., None, :]
    decay = jnp.where(lower, jnp.exp(jnp.where(lower, diff, 0.0)), 0.0)
    kk = jnp.einsum('bhncd,bhnsd->bhncs', k, k)
    m = jnp.eye(C, dtype=jnp.float32) + jnp.where(strict, beta[..., :, None] * kk * decay, 0.0)

    def solve(rhs):
        return lax.linalg.triangular_solve(m, rhs, left_side=True, lower=True, unit_diagonal=True)

    u = solve(v * beta[..., None])
    w = solve(k * (beta * jnp.exp(gc))[..., None])
    qk = jnp.where(lower, jnp.einsum('bhncd,bhnsd->bhncs', q, k) * decay, 0.0)
    qg = q * jnp.exp(gc)[..., None]
    g_last = gc[..., -1:]
    kd = k * jnp.exp(g_last - gc)[..., None]
    eg = jnp.exp(g_last[..., 0])
    xs = tuple(jnp.moveaxis(t, 2, 0) for t in (u, w, qg, qk, kd, eg))

    def step(s, inp):
        u_n, w_n, qg_n, qk_n, kd_n, eg_n = inp
        v_new = u_n - jnp.einsum('bhcd,bhde->bhce', w_n, s)
        o_n = jnp.einsum('bhcd,bhde->bhce', qg_n, s) + jnp.einsum('bhcs,bhse->bhce', qk_n, v_new)
        s = s * eg_n[..., None, None] + jnp.einsum('bhcd,bhce->bhde', kd_n, v_new)
        return s, o_n

    _, o = lax.scan(step, jnp.zeros((B, H, dk, dv), jnp.float32), xs)
    return o.transpose(1, 0, 3, 2, 4).reshape(B, S, H, dv)


def gated_deltanet_mixer(h, w_in, conv_w, a_log, dt_bias, norm_w, w_out):
    B, S, _ = h.shape
    proj = h @ w_in
    qkv, z, b_f, b_b, a_f, a_b = jnp.split(proj, ODD_SPLITS, axis=-1)
    qkv = lax.conv_general_dilated(qkv, conv_w[:, None, :].astype(qkv.dtype), window_strides=(1,),
                                   padding=[(GDN_CONV // 2, GDN_CONV // 2)],
                                   dimension_numbers=('NWC', 'WIO', 'NWC'),
                                   feature_group_count=GDN_CONV_W)
    qkv = jax.nn.silu(qkv)
    q, k, v = jnp.split(qkv, [GDN_QK_W, 2 * GDN_QK_W], axis=-1)
    rep = GDN_V_HEADS // GDN_K_HEADS
    q = jnp.repeat(l2_normalize(q.reshape(B, S, GDN_K_HEADS, GDN_DK)) * (GDN_DK ** -0.5), rep, axis=2)
    k = jnp.repeat(l2_normalize(k.reshape(B, S, GDN_K_HEADS, GDN_DK)), rep, axis=2)
    v = v.reshape(B, S, GDN_V_HEADS, GDN_DV)
    rate = jnp.exp(a_log.astype(jnp.float32))
    dtb = dt_bias.astype(jnp.float32)
    g_f = -rate[0] * jax.nn.softplus(a_f.astype(jnp.float32) + dtb[0])
    g_b = -rate[1] * jax.nn.softplus(a_b.astype(jnp.float32) + dtb[1])
    beta_f = jax.nn.sigmoid(b_f.astype(jnp.float32))
    beta_b = jax.nn.sigmoid(b_b.astype(jnp.float32))
    o_f = gdn_chunk(q, k, v, g_f, beta_f)
    o_b = gdn_chunk(jnp.flip(q, 1), jnp.flip(k, 1), jnp.flip(v, 1), jnp.flip(g_b, 1), jnp.flip(beta_b, 1))
    o = o_f + jnp.flip(o_b, 1)
    z = z.reshape(B, S, GDN_V_HEADS, GDN_DV).astype(jnp.float32)
    o = rms_norm(o, norm_w) * jax.nn.silu(z)
    return o.reshape(B, S, GDN_V_W).astype(h.dtype) @ w_out


def swiglu_ffn(h, w_gate, w_up, w_down):
    return (jax.nn.silu(h @ w_gate) * (h @ w_up)) @ w_down


def encoder(x, norm1_w, norm2_w, final_norm_w, ev_w_in, ev_w_out, ev_ret_decay_logit, ev_ret_gn_w,
            ev_gm_ln_w, ev_gm_ln_b, ev_gm_ws, ev_gm_bs, od_w_in, od_conv_w, od_a_log, od_dt_bias,
            od_norm_w, od_w_out, ffn_w_gate, ffn_w_up, ffn_w_down):
    h = x
    for layer in range(DEPTH):
        i = layer // 2
        hn = rms_norm(h, norm1_w[layer])
        if layer % 2 == 0:
            mix = retention_gmlp_mixer(hn, ev_w_in[i], ev_w_out[i], ev_ret_decay_logit[i], ev_ret_gn_w[i],
                                       ev_gm_ln_w[i], ev_gm_ln_b[i], ev_gm_ws[i], ev_gm_bs[i])
        else:
            mix = gated_deltanet_mixer(hn, od_w_in[i], od_conv_w[i], od_a_log[i], od_dt_bias[i],
                                       od_norm_w[i], od_w_out[i])
        h = h + mix
        h = h + swiglu_ffn(rms_norm(h, norm2_w[layer]), ffn_w_gate[layer], ffn_w_up[layer], ffn_w_down[layer])
    return rms_norm(h, final_norm_w)


def setup_inputs(seed: int = 0) -> dict:
    key = jax.random.key(seed)
    ks = jax.random.split(key, 24)
    f32 = jnp.float32

    def nrm(k, shape, scale):
        return jax.random.normal(k, shape, f32) * scale

    ret_base = jnp.log(2.0 ** (5.0 + jnp.arange(RET_HEADS, dtype=f32)) - 1.0)
    dt = jnp.exp(jax.random.uniform(ks[16], (N_ODD, 2, GDN_V_HEADS), f32,
                                    minval=math.log(1e-3), maxval=math.log(1e-1)))
    return {
        'x_prompt': nrm(ks[0], (BATCH, SEQ, D_MODEL), 1.0),
        'x_sample': nrm(ks[1], (DEC_BATCH, DEC_SEQ, D_MODEL), 1.0),
        'norm1_w': 1.0 + nrm(ks[2], (DEPTH, D_MODEL), 0.02),
        'norm2_w': 1.0 + nrm(ks[3], (DEPTH, D_MODEL), 0.02),
        'final_norm_w': 1.0 + nrm(ks[4], (D_MODEL,), 0.02),
        'ev_w_in': nrm(ks[5], (N_EVEN, D_MODEL, EVEN_IN_W), D_MODEL ** -0.5),
        'ev_w_out': nrm(ks[6], (N_EVEN, EVEN_MIX_W, D_MODEL), EVEN_MIX_W ** -0.5),
        'ev_ret_decay_logit': ret_base[None, None, :] + nrm(ks[7], (N_EVEN, 2, RET_HEADS), 0.05),
        'ev_ret_gn_w': 1.0 + nrm(ks[8], (N_EVEN, RET_V_W), 0.02),
        'ev_gm_ln_w': 1.0 + nrm(ks[9], (N_EVEN, GM_W), 0.02),
        'ev_gm_ln_b': nrm(ks[10], (N_EVEN, GM_W), 0.02),
        'ev_gm_ws': nrm(ks[11], (N_EVEN, GM_GROUPS, GM_CHUNK, GM_CHUNK), GM_CHUNK ** -0.5),
        'ev_gm_bs': 1.0 + nrm(ks[12], (N_EVEN, GM_GROUPS, GM_CHUNK), 0.1),
        'od_w_in': nrm(ks[13], (N_ODD, D_MODEL, ODD_IN_W), D_MODEL ** -0.5),
        'od_conv_w': nrm(ks[14], (N_ODD, GDN_CONV, GDN_CONV_W), GDN_CONV ** -0.5),
        'od_a_log': jnp.log(jax.random.uniform(ks[15], (N_ODD, 2, GDN_V_HEADS), f32, minval=1.0, maxval=16.0)),
        'od_dt_bias': dt + jnp.log(-jnp.expm1(-dt)),
        'od_norm_w': 1.0 + nrm(ks[17], (N_ODD, GDN_DV), 0.02),
        'od_w_out': nrm(ks[18], (N_ODD, GDN_V_W, D_MODEL), GDN_V_W ** -0.5),
        'ffn_w_gate': nrm(ks[19], (DEPTH, D_MODEL, D_FF), D_MODEL ** -0.5),
        'ffn_w_up': nrm(ks[20], (DEPTH, D_MODEL, D_FF), D_MODEL ** -0.5),
        'ffn_w_down': nrm(ks[21], (DEPTH, D_FF, D_MODEL), D_FF ** -0.5),
    }


def reference(x_prompt, x_sample, norm1_w, norm2_w, final_norm_w, ev_w_in, ev_w_out, ev_ret_decay_logit,
              ev_ret_gn_w, ev_gm_ln_w, ev_gm_ln_b, ev_gm_ws, ev_gm_bs, od_w_in, od_conv_w, od_a_log,
              od_dt_bias, od_norm_w, od_w_out, ffn_w_gate, ffn_w_up, ffn_w_down):
    y_prompt = encoder(x_prompt, norm1_w, norm2_w, final_norm_w, ev_w_in, ev_w_out, ev_ret_decay_logit,
                       ev_ret_gn_w, ev_gm_ln_w, ev_gm_ln_b, ev_gm_ws, ev_gm_bs, od_w_in, od_conv_w, od_a_log,
                       od_dt_bias, od_norm_w, od_w_out, ffn_w_gate, ffn_w_up, ffn_w_down)
    y_sample = encoder(x_sample, norm1_w, norm2_w, final_norm_w, ev_w_in, ev_w_out, ev_ret_decay_logit,
                       ev_ret_gn_w, ev_gm_ln_w, ev_gm_ln_b, ev_gm_ws, ev_gm_bs, od_w_in, od_conv_w, od_a_log,
                       od_dt_bias, od_norm_w, od_w_out, ffn_w_gate, ffn_w_up, ffn_w_down)
    return (y_prompt, y_sample)
```

```python
import functools
import math

import jax
import jax.numpy as jnp
from jax import lax
from jax.experimental import pallas as pl
from jax.experimental.pallas import tpu as pltpu

F32 = jnp.float32
BF16 = jnp.bfloat16

D_MODEL = 2048
NORM_EPS = 1e-6
L2_EPS = 1e-6
RET_HEADS = 8
RET_DK = 64
RET_DV = 128
RET_CHUNK = 128
ROPE_BASE = 10000.0
RET_QK_W = RET_HEADS * RET_DK
RET_V_W = RET_HEADS * RET_DV
GM_GROUPS = 8
GM_GROUP_DIM = 128
GM_W = GM_GROUPS * GM_GROUP_DIM
EVEN_IN_W = 2 * RET_QK_W + 2 * RET_V_W + 2 * GM_W
EVEN_MIX_W = RET_V_W + GM_W
GDN_K_HEADS = 16
GDN_V_HEADS = 32
GDN_DK = 128
GDN_DV = 128
GDN_CONV = 5
GDN_QK_W = GDN_K_HEADS * GDN_DK
GDN_V_W = GDN_V_HEADS * GDN_DV
GDN_CONV_W = 2 * GDN_QK_W + GDN_V_W
GDN_MAIN_W = GDN_CONV_W + GDN_V_W
GDN_GATE_W = 4 * GDN_V_HEADS
GDN_BLOCK = 128
LANES = 128
HALO = 16
VMEM_LIMIT = 56 * 1024 * 1024


def _cparams(sem):
    return pltpu.CompilerParams(dimension_semantics=sem, vmem_limit_bytes=VMEM_LIMIT)


def _rms(x, w):
    return x * lax.rsqrt(jnp.mean(x * x, axis=-1, keepdims=True) + NORM_EPS) * w


def _dot(a, b):
    return jnp.dot(a, b, preferred_element_type=F32)


def _norm_matmul_kernel(x_ref, nw_ref, w_ref, o_ref, hn_ref):
    @pl.when(pl.program_id(1) == 0)
    def _():
        hn_ref[...] = _rms(x_ref[...], nw_ref[...]).astype(BF16)

    o_ref[...] = _dot(hn_ref[...], w_ref[...]).astype(o_ref.dtype)


def norm_matmul(x, nw, w, out_dtype, tm, tn):
    T, D = x.shape
    N = w.shape[1]
    return pl.pallas_call(
        _norm_matmul_kernel,
        grid=(T // tm, N // tn),
        in_specs=[pl.BlockSpec((tm, D), lambda i, j: (i, 0)),
                  pl.BlockSpec((1, D), lambda i, j: (0, 0)),
                  pl.BlockSpec((D, tn), lambda i, j: (0, j))],
        out_specs=pl.BlockSpec((tm, tn), lambda i, j: (i, j)),
        out_shape=jax.ShapeDtypeStruct((T, N), out_dtype),
        scratch_shapes=[pltpu.VMEM((tm, D), BF16)],
        compiler_params=_cparams(("parallel", "arbitrary")),
        name="norm_matmul",
    )(x, nw.reshape(1, D), w)


def _matmul_residual_kernel(a_ref, w_ref, h_ref, o_ref):
    o_ref[...] = h_ref[...] + _dot(a_ref[...], w_ref[...])


def matmul_residual(a, w, h, tm, tn):
    T, K = a.shape
    N = w.shape[1]
    return pl.pallas_call(
        _matmul_residual_kernel,
        grid=(T // tm, N // tn),
        in_specs=[pl.BlockSpec((tm, K), lambda i, j: (i, 0)),
                  pl.BlockSpec((K, tn), lambda i, j: (0, j)),
                  pl.BlockSpec((tm, tn), lambda i, j: (i, j))],
        out_specs=pl.BlockSpec((tm, tn), lambda i, j: (i, j)),
        out_shape=jax.ShapeDtypeStruct((T, N), F32),
        compiler_params=_cparams(("parallel", "arbitrary")),
        name="matmul_residual",
    )(a, w, h)


def _ffn_kernel(h_ref, nw_ref, wg_ref, wu_ref, wd_ref, fw_ref, o_ref, hn_ref, acc_ref, *, final_norm):
    f = pl.program_id(1)

    @pl.when(f == 0)
    def _():
        hn_ref[...] = _rms(h_ref[...], nw_ref[...]).astype(BF16)
        acc_ref[...] = jnp.zeros_like(acc_ref)

    hn = hn_ref[...]
    gate = _dot(hn, wg_ref[...])
    up = _dot(hn, wu_ref[...])
    act = (gate * jax.nn.sigmoid(gate) * up).astype(BF16)
    acc_ref[...] += _dot(act, wd_ref[...])

    @pl.when(f == pl.num_programs(1) - 1)
    def _():
        y = h_ref[...] + acc_ref[...]
        if final_norm:
            y = _rms(y, fw_ref[...])
        o_ref[...] = y


def ffn(h, nw, wg, wu, wd, fw, final_norm, tm, tf):
    T, D = h.shape
    F = wg.shape[1]
    return pl.pallas_call(
        functools.partial(_ffn_kernel, final_norm=final_norm),
        grid=(T // tm, F // tf),
        in_specs=[pl.BlockSpec((tm, D), lambda i, f: (i, 0)),
                  pl.BlockSpec((1, D), lambda i, f: (0, 0)),
                  pl.BlockSpec((D, tf), lambda i, f: (0, f)),
                  pl.BlockSpec((D, tf), lambda i, f: (0, f)),
                  pl.BlockSpec((tf, D), lambda i, f: (f, 0)),
                  pl.BlockSpec((1, D), lambda i, f: (0, 0))],
        out_specs=pl.BlockSpec((tm, D), lambda i, f: (i, 0)),
        out_shape=jax.ShapeDtypeStruct((T, D), F32),
        scratch_shapes=[pltpu.VMEM((tm, D), BF16), pltpu.VMEM((tm, D), F32)],
        compiler_params=_cparams(("parallel", "arbitrary")),
        name="ffn",
    )(h, nw.reshape(1, D), wg, wu, wd, fw.reshape(1, D))


def _rotary(x, cos, sin_lo, sin_hi):
    n = x.shape[-1]
    half = RET_DK // 2
    return x * cos + pltpu.roll(x, n - half, 1) * sin_lo + pltpu.roll(x, half, 1) * sin_hi


def _standardize(x):
    xc = x - jnp.mean(x, axis=-1, keepdims=True)
    return xc * lax.rsqrt(jnp.mean(xc * xc, axis=-1, keepdims=True) + NORM_EPS)


def _retgmlp_kernel(q_ref, k_ref, v_ref, g_ref, gu_ref, gv_ref,
                    cosq_ref, slq_ref, shq_ref, cosk_ref, slk_ref, shk_ref,
                    dmat_ref, dq_ref, dk_ref, cdec_ref, hmask_ref,
                    gnw_ref, lnw_ref, lnb_ref, ws_ref, bs_ref,
                    o_ref, sf_ref, sb_ref, sball_ref, *, n_chunks):
    phase = pl.program_id(1)
    n = pl.program_id(2)
    C = RET_CHUNK

    def head_k_t(k_scaled, h):
        p = h // 2
        return k_scaled[:, p * LANES:(p + 1) * LANES].T.astype(BF16)

    @pl.when(phase == 0)
    def _():
        @pl.when(n == 0)
        def _():
            sb_ref[...] = jnp.zeros_like(sb_ref)

        c = n_chunks - 1 - n
        k_r = _rotary(k_ref[0].astype(F32), cosk_ref[...], slk_ref[...], shk_ref[...])
        kb = k_r * dk_ref[1]
        sball_ref[c] = sb_ref[...].astype(BF16)
        for h in range(RET_HEADS):
            if h % 2 == 0:
                kbt = head_k_t(kb, h)
            v_h = v_ref[0, :, h * RET_DV:(h + 1) * RET_DV]
            sb_ref[h] = sb_ref[h] * cdec_ref[1, h:h + 1, :] + _dot(kbt, v_h)

    @pl.when(phase == 1)
    def _():
        @pl.when(n == 0)
        def _():
            sf_ref[...] = jnp.zeros_like(sf_ref)

        q_r = _rotary(q_ref[0].astype(F32), cosq_ref[...], slq_ref[...], shq_ref[...])
        k_r = _rotary(k_ref[0].astype(F32), cosk_ref[...], slk_ref[...], shk_ref[...])
        kf = k_r * dk_ref[0]
        qf = q_r * dq_ref[0]
        qb = q_r * dq_ref[1]
        k_bf = k_r.astype(BF16)
        for h in range(RET_HEADS):
            p = h // 2
            sl = slice(p * LANES, (p + 1) * LANES)
            m = hmask_ref[h % 2:h % 2 + 1, :]
            q_m = (q_r[:, sl] * m).astype(BF16)
            v_h = v_ref[0, :, h * RET_DV:(h + 1) * RET_DV]
            scores = lax.dot_general(q_m, k_bf[:, sl], (((1,), (1,)), ((), ())),
                                     preferred_element_type=F32) * dmat_ref[h]
            out = _dot(scores.astype(BF16), v_h)
            out += _dot((qf[:, sl] * m).astype(BF16), sf_ref[h].astype(BF16))
            out += _dot((qb[:, sl] * m).astype(BF16), sball_ref[n, h])
            if h % 2 == 0:
                kft = head_k_t(kf, h)
            sf_ref[h] = sf_ref[h] * cdec_ref[0, h:h + 1, :] + _dot(kft, v_h)
            vs = slice(h * RET_DV, (h + 1) * RET_DV)
            y = _standardize(out) * gnw_ref[:, vs]
            gate = g_ref[0, :, vs].astype(F32)
            o_ref[0, :, vs] = (gate * jax.nn.sigmoid(gate) * y).astype(o_ref.dtype)

        gu = jax.nn.gelu(gu_ref[0].astype(F32))
        gv = _standardize(jax.nn.gelu(gv_ref[0].astype(F32))) * lnw_ref[...] + lnb_ref[...]
        gv = gv.astype(BF16)
        for g in range(GM_GROUPS):
            gs = slice(g * GM_GROUP_DIM, (g + 1) * GM_GROUP_DIM)
            mixed = _dot(ws_ref[g], gv[:, gs]) + bs_ref[g]
            o_ref[0, :, RET_V_W + g * GM_GROUP_DIM:RET_V_W + (g + 1) * GM_GROUP_DIM] = (
                gu[:, gs] * mixed).astype(o_ref.dtype)


def _retention_tables(S, decay_logit):
    half = RET_DK // 2
    inv = ROPE_BASE ** (-jnp.arange(half, dtype=F32) / half)
    ang = jnp.arange(S, dtype=F32)[:, None] * inv[None, :]
    cos, sin = jnp.cos(ang), jnp.sin(ang)
    zeros = jnp.zeros_like(sin)
    reps = RET_QK_W // RET_DK
    cos_t = jnp.tile(jnp.concatenate([cos, cos], axis=1), (1, reps))
    sin_lo = jnp.tile(jnp.concatenate([-sin, zeros], axis=1), (1, reps))
    sin_hi = jnp.tile(jnp.concatenate([zeros, sin], axis=1), (1, reps))
    C = RET_CHUNK
    log_gamma = jax.nn.log_sigmoid(decay_logit.astype(F32))
    lf, lb = log_gamma[0], log_gamma[1]
    idx = jnp.arange(C, dtype=F32)
    diff = idx[:, None] - idx[None, :]
    adiff = jnp.abs(diff)[None]
    dmat = jnp.where((diff >= 0)[None], jnp.exp(adiff * lf[:, None, None]),
                     jnp.exp(adiff * lb[:, None, None]))

    def lanes(t):
        return jnp.repeat(t, RET_DK, axis=1)

    dq = jnp.stack([lanes(jnp.exp((idx + 1)[:, None] * lf[None, :])),
                    lanes(jnp.exp((C - 1 - idx)[:, None] * lb[None, :]))])
    dk = jnp.stack([lanes(jnp.exp((C - 1 - idx)[:, None] * lf[None, :])),
                    lanes(jnp.exp((idx + 1)[:, None] * lb[None, :]))])
    cdec = jnp.broadcast_to(jnp.exp(C * log_gamma)[:, :, None], (2, RET_HEADS, RET_DV))
    lane = jnp.arange(LANES)
    hmask = jnp.stack([(lane < RET_DK), (lane >= RET_DK)]).astype(F32) * (RET_DK ** -0.5)
    return cos_t, sin_lo, sin_hi, dmat, dq, dk, cdec, hmask


def retention_gmlp(proj, decay_logit, gn_w, ln_w, ln_b, ws, bs):
    B, S, _ = proj.shape
    C = RET_CHUNK
    N = S // C
    cos_t, sin_lo, sin_hi, dmat, dq, dk, cdec, hmask = _retention_tables(S, decay_logit)
    bs_b = jnp.broadcast_to(bs.astype(F32)[:, :, None], (GM_GROUPS, C, GM_GROUP_DIM))

    def kv_chunk(ph, n):
        return ph * n + (1 - ph) * (N - 1 - n)

    qw, vw = RET_QK_W, RET_V_W
    col = lambda width, start: start // width
    q_spec = pl.BlockSpec((1, C, qw), lambda b, ph, n: (b, ph * n, col(qw, 0)))
    k_spec = pl.BlockSpec((1, C, qw), lambda b, ph, n: (b, kv_chunk(ph, n), col(qw, qw)))
    v_spec = pl.BlockSpec((1, C, vw), lambda b, ph, n: (b, kv_chunk(ph, n), col(vw, 2 * qw)))
    g_spec = pl.BlockSpec((1, C, vw), lambda b, ph, n: (b, ph * n, col(vw, 2 * qw + vw)))
    gu_spec = pl.BlockSpec((1, C, vw), lambda b, ph, n: (b, ph * n, col(vw, 2 * qw + 2 * vw)))
    gv_spec = pl.BlockSpec((1, C, vw), lambda b, ph, n: (b, ph * n, col(vw, 2 * qw + 3 * vw)))
    tq_spec = pl.BlockSpec((C, qw), lambda b, ph, n: (ph * n, 0))
    tk_spec = pl.BlockSpec((C, qw), lambda b, ph, n: (kv_chunk(ph, n), 0))

    def const(shape):
        nd = len(shape)
        return pl.BlockSpec(shape, lambda b, ph, n: (0,) * nd)

    return pl.pallas_call(
        functools.partial(_retgmlp_kernel, n_chunks=N),
        grid=(B, 2, N),
        in_specs=[q_spec, k_spec, v_spec, g_spec, gu_spec, gv_spec,
                  tq_spec, tq_spec, tq_spec, tk_spec, tk_spec, tk_spec,
                  const(dmat.shape), const(dq.shape), const(dk.shape), const(cdec.shape), const(hmask.shape),
                  const((1, vw)), const((1, GM_W)), const((1, GM_W)), const(ws.shape), const(bs_b.shape)],
        out_specs=pl.BlockSpec((1, C, EVEN_MIX_W), lambda b, ph, n: (b, ph * n, 0)),
        out_shape=jax.ShapeDtypeStruct((B, S, EVEN_MIX_W), BF16),
        scratch_shapes=[pltpu.VMEM((RET_HEADS, LANES, RET_DV), F32),
                        pltpu.VMEM((RET_HEADS, LANES, RET_DV), F32),
                        pltpu.VMEM((N, RET_HEADS, LANES, RET_DV), BF16)],
        compiler_params=_cparams(("parallel", "arbitrary", "arbitrary")),
        name="retention_gmlp",
    )(proj, proj, proj, proj, proj, proj, cos_t, sin_lo, sin_hi, cos_t, sin_lo, sin_hi,
      dmat, dq, dk, cdec, hmask, gn_w.reshape(1, vw).astype(F32), ln_w.reshape(1, GM_W).astype(F32),
      ln_b.reshape(1, GM_W).astype(F32), ws.astype(BF16), bs_b)


def _conv_kernel(xm_ref, xp_ref, xn_ref, cw_ref, cs_ref, o_ref, pad_ref, *, ts, l2):
    i = pl.program_id(1)
    last = pl.num_programs(1) - 1
    pad_ref[0:HALO, :] = jnp.where(i > 0, xp_ref[0].astype(F32), 0.0)
    pad_ref[HALO:HALO + ts, :] = xm_ref[0].astype(F32)
    pad_ref[HALO + ts:2 * HALO + ts, :] = jnp.where(i < last, xn_ref[0].astype(F32), 0.0)
    base = HALO - GDN_CONV // 2
    acc = cw_ref[0:1, :] * pad_ref[base:base + ts, :]
    for w in range(1, GDN_CONV):
        acc = acc + cw_ref[w:w + 1, :] * pad_ref[base + w:base + w + ts, :]
    y = acc * jax.nn.sigmoid(acc)
    if l2:
        tc = y.shape[1]
        for hd in range(tc // GDN_DK):
            hs = slice(hd * GDN_DK, (hd + 1) * GDN_DK)
            yh = y[:, hs]
            yh = yh * lax.rsqrt(jnp.sum(yh * yh, axis=-1, keepdims=True) + L2_EPS) * cs_ref[:, hs]
            o_ref[0, :, hs] = yh.astype(o_ref.dtype)
    else:
        o_ref[0] = y.astype(o_ref.dtype)


def conv_silu(proj, conv_w, colscale, col0, width, l2, ts, tc):
    B, S, _ = proj.shape
    cb0 = col0 // tc
    nt = S // ts
    hb = ts // HALO
    return pl.pallas_call(
        functools.partial(_conv_kernel, ts=ts, l2=l2),
        grid=(B, nt, width // tc),
        in_specs=[pl.BlockSpec((1, ts, tc), lambda b, i, j: (b, i, cb0 + j)),
                  pl.BlockSpec((1, HALO, tc), lambda b, i, j: (b, jnp.maximum(i * hb - 1, 0), cb0 + j)),
                  pl.BlockSpec((1, HALO, tc), lambda b, i, j: (b, jnp.minimum((i + 1) * hb, S // HALO - 1), cb0 + j)),
                  pl.BlockSpec((GDN_CONV, tc), lambda b, i, j: (0, cb0 + j)),
                  pl.BlockSpec((1, tc), lambda b, i, j: (0, j))],
        out_specs=pl.BlockSpec((1, ts, tc), lambda b, i, j: (b, i, j)),
        out_shape=jax.ShapeDtypeStruct((B, S, width), BF16),
        scratch_shapes=[pltpu.VMEM((ts + 2 * HALO, tc), F32)],
        compiler_params=_cparams(("parallel", "arbitrary", "arbitrary")),
        name="conv_silu_l2" if l2 else "conv_silu",
    )(proj, proj, proj, conv_w, colscale)


def _split3(x):
    hi = x.astype(BF16)
    r = x - hi.astype(F32)
    mid = r.astype(BF16)
    lo = (r - mid.astype(F32)).astype(BF16)
    return hi, mid, lo


def _gates_kernel(x_ref, dtb_ref, rate_ref, o_ref, *, ts):
    G = GDN_BLOCK
    row = lax.broadcasted_iota(jnp.int32, (G, G), 0)
    col = lax.broadcasted_iota(jnp.int32, (G, G), 1)
    lower = (row >= col).astype(BF16)
    upper = (row <= col).astype(BF16)
    lane = lax.broadcasted_iota(jnp.int32, (G, GDN_GATE_W), 1)
    for c in range(ts // G):
        x = x_ref[c * G:(c + 1) * G, :]
        beta = jax.nn.sigmoid(x)
        z = x + dtb_ref[...]
        softplus = jnp.maximum(z, 0.0) + jnp.log1p(jnp.exp(-jnp.abs(z)))
        g = -rate_ref[...] * softplus
        hi, mid, lo = _split3(g)
        prefix = _dot(lower, hi) + _dot(lower, mid) + _dot(lower, lo)
        suffix = _dot(upper, hi) + _dot(upper, mid) + _dot(upper, lo)
        out = jnp.where(lane < 2 * GDN_V_HEADS, beta, jnp.where(lane < 3 * GDN_V_HEADS, prefix, suffix))
        o_ref[0, :, c * G:(c + 1) * G] = out.T


def gdn_gates(gproj, a_log, dt_bias, B, S, ts):
    zeros = jnp.zeros((2 * GDN_V_HEADS,), F32)
    dtb = jnp.concatenate([zeros, dt_bias.astype(F32).reshape(-1)]).reshape(1, GDN_GATE_W)
    rate = jnp.concatenate([zeros, jnp.exp(a_log.astype(F32)).reshape(-1)]).reshape(1, GDN_GATE_W)
    nt = S // ts
    return pl.pallas_call(
        functools.partial(_gates_kernel, ts=ts),
        grid=(B, nt),
        in_specs=[pl.BlockSpec((ts, GDN_GATE_W), lambda b, i: (b * nt + i, 0)),
                  pl.BlockSpec((1, GDN_GATE_W), lambda b, i: (0, 0)),
                  pl.BlockSpec((1, GDN_GATE_W), lambda b, i: (0, 0))],
        out_specs=pl.BlockSpec((1, GDN_GATE_W, ts), lambda b, i: (b, 0, i)),
        out_shape=jax.ShapeDtypeStruct((B, GDN_GATE_W, S), F32),
        compiler_params=_cparams(("parallel", "arbitrary")),
        name="gdn_gates",
    )(gproj, dtb, rate)


def _unit_triangular_inverse(a):
    G = a.shape[0]
    row = lax.broadcasted_iota(jnp.int32, (G, G), 0)
    col = lax.broadcasted_iota(jnp.int32, (G, G), 1)
    eye = (row == col).astype(F32)

    def siblings(k):
        return ((row >> (k + 1)) == (col >> (k + 1))) & ((row >> k) != (col >> k))

    td = eye - jnp.where(siblings(0), a, 0.0)
    for k in range(1, int(math.log2(G))):
        a_off = jnp.where(siblings(k), a, 0.0).astype(BF16)
        x = _dot(a_off, td.astype(BF16))
        td = td - _dot(td.astype(BF16), x.astype(BF16))
    return td


def _gdn_kernel(q_ref, k_ref, kt_ref, v_ref, z_ref, g_ref, nw_ref, o_ref,
                w_s, qg_s, u_s, qkd_s, kdt_s, eg_s, o_s, *, n_blocks):
    G = GDN_BLOCK
    row = lax.broadcasted_iota(jnp.int32, (G, G), 0)
    col = lax.broadcasted_iota(jnp.int32, (G, G), 1)

    def prepare(n, carry):
        t0 = pl.multiple_of(n * G, G)
        tok = pl.ds(t0, G)
        qc = q_ref[0, tok, :]
        kc = k_ref[0, tok, :]
        ktc = kt_ref[0, 0, :, tok]
        vc = v_ref[0, tok, :].astype(F32)
        kk = _dot(kc, ktc)
        qk = _dot(qc, ktc)
        qf = qc.astype(F32)
        kf = kc.astype(F32)
        for d in range(2):
            beta_r = g_ref[0, 0, d:d + 1, tok]
            gc_r = g_ref[0, 0, 2 + d:3 + d, tok]
            beta_c = jnp.broadcast_to(beta_r, (G, G)).T
            gc_c = jnp.broadcast_to(gc_r, (G, G)).T
            tri = (row >= col) if d == 0 else (row <= col)
            strict = (row > col) if d == 0 else (row < col)
            decay = jnp.where(tri, jnp.exp(jnp.where(tri, gc_c - gc_r, 0.0)), 0.0)
            a = jnp.where(strict, beta_c * kk * decay, 0.0)
            t_inv = _unit_triangular_inverse(a).astype(BF16)
            e_c = jnp.exp(gc_c)
            rhs = jnp.concatenate([(beta_c * vc).astype(BF16), (beta_c * e_c * kf).astype(BF16)], axis=1)
            uw = _dot(t_inv, rhs)
            g_last = gc_r[:, G - 1:G] if d == 0 else gc_r[:, 0:1]
            u_s[d, tok, :] = uw[:, :GDN_DV]
            w_s[d, tok, :] = uw[:, GDN_DV:].astype(BF16)
            qg_s[d, tok, :] = (qf * e_c).astype(BF16)
            qkd_s[d, tok, :] = (qk * decay).astype(BF16)
            kdt_s[d, :, tok] = (ktc.astype(F32) * jnp.exp(g_last - gc_r)).astype(BF16)
            eg_s[d, pl.ds(pl.multiple_of(n * 8, 8), 8), :] = jnp.broadcast_to(jnp.exp(g_last), (8, GDN_DV))
        return carry

    lax.fori_loop(0, n_blocks, prepare, 0)

    def scan(n, states):
        new_states = []
        for d in range(2):
            c = n if d == 0 else n_blocks - 1 - n
            t0 = pl.multiple_of(c * G, G)
            tok = pl.ds(t0, G)
            s = states[d]
            s_bf = s.astype(BF16)
            v_new = u_s[d, tok, :] - _dot(w_s[d, tok, :], s_bf)
            v_bf = v_new.astype(BF16)
            o = _dot(qg_s[d, tok, :], s_bf) + _dot(qkd_s[d, tok, :], v_bf)
            eg = eg_s[d, pl.ds(pl.multiple_of(c * 8, 8), 1), :]
            new_states.append(s * eg + _dot(kdt_s[d, :, tok], v_bf))
            o_s[d, tok, :] = o
        return tuple(new_states)

    zero = jnp.zeros((GDN_DK, GDN_DV), F32)
    lax.fori_loop(0, n_blocks, scan, (zero, zero))

    def finish(n, carry):
        tok = pl.ds(pl.multiple_of(n * G, G), G)
        o = o_s[0, tok, :] + o_s[1, tok, :]
        z = z_ref[0, tok, :].astype(F32)
        o_ref[0, tok, :] = (_rms(o, nw_ref[...]) * (z * jax.nn.sigmoid(z))).astype(o_ref.dtype)
        return carry

    lax.fori_loop(0, n_blocks, finish, 0)


def gdn_scan(qk, kt, v, proj, gates, norm_w):
    B, S, _ = v.shape
    nb = S // GDN_BLOCK
    rep = GDN_V_HEADS // GDN_K_HEADS
    zb0 = GDN_CONV_W // GDN_DV
    tile = lambda f: pl.BlockSpec((1, S, LANES), f)
    return pl.pallas_call(
        functools.partial(_gdn_kernel, n_blocks=nb),
        grid=(B, GDN_V_HEADS),
        in_specs=[tile(lambda b, h: (b, 0, h // rep)),
                  tile(lambda b, h: (b, 0, GDN_K_HEADS + h // rep)),
                  pl.BlockSpec((1, 1, GDN_DK, S), lambda b, h: (b, h // rep, 0, 0)),
                  tile(lambda b, h: (b, 0, h)),
                  tile(lambda b, h: (b, 0, zb0 + h)),
                  pl.BlockSpec((1, 1, 4, S), lambda b, h: (b, h, 0, 0)),
                  pl.BlockSpec((1, GDN_DV), lambda b, h: (0, 0))],
        out_specs=tile(lambda b, h: (b, 0, h)),
        out_shape=jax.ShapeDtypeStruct((B, S, GDN_V_W), BF16),
        scratch_shapes=[pltpu.VMEM((2, S, GDN_DK), BF16),
                        pltpu.VMEM((2, S, GDN_DK), BF16),
                        pltpu.VMEM((2, S, GDN_DV), F32),
                        pltpu.VMEM((2, S, GDN_BLOCK), BF16),
                        pltpu.VMEM((2, GDN_DK, S), BF16),
                        pltpu.VMEM((2, 8 * nb, GDN_DV), F32),
                        pltpu.VMEM((2, S, GDN_DV), F32)],
        compiler_params=_cparams(("parallel", "arbitrary")),
        name="gdn_scan",
    )(qk, qk, kt, v, proj, gates, norm_w.reshape(1, GDN_DV).astype(F32))


def _tile(total, want):
    t = min(want, total)
    while total % t:
        t //= 2
    return t


def _encoder(x, p):
    B, S, D = x.shape
    T = B * S
    h = x.reshape(T, D)
    tm = _tile(T, 1024)

    proj = norm_matmul(h, p["norm1_w"][0], p["ev_w_in"], BF16, tm, 512)
    mix = retention_gmlp(proj.reshape(B, S, EVEN_IN_W), p["ev_ret_decay_logit"], p["ev_ret_gn_w"],
                         p["ev_gm_ln_w"], p["ev_gm_ln_b"], p["ev_gm_ws"], p["ev_gm_bs"])
    h = matmul_residual(mix.reshape(T, EVEN_MIX_W), p["ev_w_out"], h, tm, 512)
    h = ffn(h, p["norm2_w"][0], p["ffn_w_gate"][0], p["ffn_w_up"][0], p["ffn_w_down"][0],
            p["final_norm_w"], False, _tile(T, 512), 512)

    proj = norm_matmul(h, p["norm1_w"][1], p["od_w_main"], BF16, tm, 512).reshape(B, S, GDN_MAIN_W)
    gproj = norm_matmul(h, p["norm1_w"][1], p["od_w_gate"], F32, tm, GDN_GATE_W)
    ts = _tile(S, 512)
    qk = conv_silu(proj, p["od_conv_w"], p["od_qk_scale"], 0, 2 * GDN_QK_W, True, ts, 512)
    v = conv_silu(proj, p["od_conv_w"], p["od_qk_scale"], 2 * GDN_QK_W, GDN_V_W, False, ts, 512)
    gates = gdn_gates(gproj, p["od_a_log"], p["od_dt_bias"], B, S, ts)
    gates = gates.reshape(B, 4, GDN_V_HEADS, S).transpose(0, 2, 1, 3)
    kt = qk[:, :, GDN_QK_W:].reshape(B, S, GDN_K_HEADS, GDN_DK).transpose(0, 2, 3, 1)
    o = gdn_scan(qk, kt, v, proj, gates, p["od_norm_w"])
    h = matmul_residual(o.reshape(T, GDN_V_W), p["od_w_out"], h, tm, 512)
    h = ffn(h, p["norm2_w"][1], p["ffn_w_gate"][1], p["ffn_w_up"][1], p["ffn_w_down"][1],
            p["final_norm_w"], True, _tile(T, 512), 512)
    return h.reshape(B, S, D)


def kernel(x_prompt, x_sample, norm1_w, norm2_w, final_norm_w, ev_w_in, ev_w_out, ev_ret_decay_logit, ev_ret_gn_w, ev_gm_ln_w, ev_gm_ln_b, ev_gm_ws, ev_gm_bs, od_w_in, od_conv_w, od_a_log, od_dt_bias, od_norm_w, od_w_out, ffn_w_gate, ffn_w_up, ffn_w_down):
    qk_scale = jnp.concatenate([jnp.full((GDN_QK_W,), GDN_DK ** -0.5, F32), jnp.ones((GDN_QK_W,), F32)])
    p = {
        "norm1_w": norm1_w.astype(F32), "norm2_w": norm2_w.astype(F32), "final_norm_w": final_norm_w.astype(F32),
        "ev_w_in": ev_w_in[0].astype(BF16), "ev_w_out": ev_w_out[0].astype(BF16),
        "ev_ret_decay_logit": ev_ret_decay_logit[0], "ev_ret_gn_w": ev_ret_gn_w[0],
        "ev_gm_ln_w": ev_gm_ln_w[0], "ev_gm_ln_b": ev_gm_ln_b[0], "ev_gm_ws": ev_gm_ws[0], "ev_gm_bs": ev_gm_bs[0],
        "od_w_main": od_w_in[0, :, :GDN_MAIN_W].astype(BF16), "od_w_gate": od_w_in[0, :, GDN_MAIN_W:].astype(BF16),
        "od_conv_w": od_conv_w[0].astype(F32), "od_qk_scale": qk_scale.reshape(1, -1),
        "od_a_log": od_a_log[0], "od_dt_bias": od_dt_bias[0], "od_norm_w": od_norm_w[0],
        "od_w_out": od_w_out[0].astype(BF16),
        "ffn_w_gate": ffn_w_gate.astype(BF16), "ffn_w_up": ffn_w_up.astype(BF16), "ffn_w_down": ffn_w_down.astype(BF16),
    }
    return (_encoder(x_prompt, p), _encoder(x_sample, p))
```

```python
import functools
import math

import jax
import jax.numpy as jnp
from jax import lax
from jax.experimental import pallas as pl
from jax.experimental.pallas import tpu as pltpu

F32 = jnp.float32
BF16 = jnp.bfloat16

D_MODEL = 2048
NORM_EPS = 1e-6
L2_EPS = 1e-6
RET_HEADS = 8
RET_DK = 64
RET_DV = 128
RET_CHUNK = 128
ROPE_BASE = 10000.0
RET_QK_W = RET_HEADS * RET_DK
RET_V_W = RET_HEADS * RET_DV
GM_GROUPS = 8
GM_GROUP_DIM = 128
GM_W = GM_GROUPS * GM_GROUP_DIM
EVEN_IN_W = 2 * RET_QK_W + 2 * RET_V_W + 2 * GM_W
EVEN_MIX_W = RET_V_W + GM_W
GDN_K_HEADS = 16
GDN_V_HEADS = 32
GDN_DK = 128
GDN_DV = 128
GDN_CONV = 5
GDN_QK_W = GDN_K_HEADS * GDN_DK
GDN_V_W = GDN_V_HEADS * GDN_DV
GDN_CONV_W = 2 * GDN_QK_W + GDN_V_W
GDN_MAIN_W = GDN_CONV_W + GDN_V_W
GDN_GATE_W = 4 * GDN_V_HEADS
GDN_BLOCK = 128
LANES = 128
HALO = 16
VMEM_LIMIT = 56 * 1024 * 1024


def _cparams(sem):
    return pltpu.CompilerParams(dimension_semantics=sem, vmem_limit_bytes=VMEM_LIMIT)


def _rms(x, w):
    return x * lax.rsqrt(jnp.mean(x * x, axis=-1, keepdims=True) + NORM_EPS) * w


def _dot(a, b):
    return jnp.dot(a, b, preferred_element_type=F32)


def _norm_matmul_kernel(x_ref, nw_ref, w_ref, o_ref, hn_ref):
    @pl.when(pl.program_id(1) == 0)
    def _():
        hn_ref[...] = _rms(x_ref[...], nw_ref[...]).astype(BF16)

    o_ref[...] = _dot(hn_ref[...], w_ref[...]).astype(o_ref.dtype)


def norm_matmul(x, nw, w, out_dtype, tm, tn):
    T, D = x.shape
    N = w.shape[1]
    return pl.pallas_call(
        _norm_matmul_kernel,
        grid=(T // tm, N // tn),
        in_specs=[pl.BlockSpec((tm, D), lambda i, j: (i, 0)),
                  pl.BlockSpec((1, D), lambda i, j: (0, 0)),
                  pl.BlockSpec((D, tn), lambda i, j: (0, j))],
        out_specs=pl.BlockSpec((tm, tn), lambda i, j: (i, j)),
        out_shape=jax.ShapeDtypeStruct((T, N), out_dtype),
        scratch_shapes=[pltpu.VMEM((tm, D), BF16)],
        compiler_params=_cparams(("parallel", "arbitrary")),
        name="norm_matmul",
    )(x, nw.reshape(1, D), w)


def _matmul_residual_kernel(a_ref, w_ref, h_ref, o_ref):
    o_ref[...] = h_ref[...] + _dot(a_ref[...], w_ref[...])


def matmul_residual(a, w, h, tm, tn):
    T, K = a.shape
    N = w.shape[1]
    return pl.pallas_call(
        _matmul_residual_kernel,
        grid=(T // tm, N // tn),
        in_specs=[pl.BlockSpec((tm, K), lambda i, j: (i, 0)),
                  pl.BlockSpec((K, tn), lambda i, j: (0, j)),
                  pl.BlockSpec((tm, tn), lambda i, j: (i, j))],
        out_specs=pl.BlockSpec((tm, tn), lambda i, j: (i, j)),
        out_shape=jax.ShapeDtypeStruct((T, N), F32),
        compiler_params=_cparams(("parallel", "arbitrary")),
        name="matmul_residual",
    )(a, w, h)


def _ffn_kernel(h_ref, nw_ref, wg_ref, wu_ref, wd_ref, fw_ref, o_ref, hn_ref, acc_ref, *, final_norm):
    f = pl.program_id(1)

    @pl.when(f == 0)
    def _():
        hn_ref[...] = _rms(h_ref[...], nw_ref[...]).astype(BF16)
        acc_ref[...] = jnp.zeros_like(acc_ref)

    hn = hn_ref[...]
    gate = _dot(hn, wg_ref[...])
    up = _dot(hn, wu_ref[...])
    act = (gate * jax.nn.sigmoid(gate) * up).astype(BF16)
    acc_ref[...] += _dot(act, wd_ref[...])

    @pl.when(f == pl.num_programs(1) - 1)
    def _():
        y = h_ref[...] + acc_ref[...]
        if final_norm:
            y = _rms(y, fw_ref[...])
        o_ref[...] = y


def ffn(h, nw, wg, wu, wd, fw, final_norm, tm, tf):
    T, D = h.shape
    F = wg.shape[1]
    return pl.pallas_call(
        functools.partial(_ffn_kernel, final_norm=final_norm),
        grid=(T // tm, F // tf),
        in_specs=[pl.BlockSpec((tm, D), lambda i, f: (i, 0)),
                  pl.BlockSpec((1, D), lambda i, f: (0, 0)),
                  pl.BlockSpec((D, tf), lambda i, f: (0, f)),
                  pl.BlockSpec((D, tf), lambda i, f: (0, f)),
                  pl.BlockSpec((tf, D), lambda i, f: (f, 0)),
                  pl.BlockSpec((1, D), lambda i, f: (0, 0))],
        out_specs=pl.BlockSpec((tm, D), lambda i, f: (i, 0)),
        out_shape=jax.ShapeDtypeStruct((T, D), F32),
        scratch_shapes=[pltpu.VMEM((tm, D), BF16), pltpu.VMEM((tm, D), F32)],
        compiler_params=_cparams(("parallel", "arbitrary")),
        name="ffn",
    )(h, nw.reshape(1, D), wg, wu, wd, fw.reshape(1, D))


def _rotary(x, cos, sin_lo, sin_hi):
    n = x.shape[-1]
    half = RET_DK // 2
    return x * cos + pltpu.roll(x, n - half, 1) * sin_lo + pltpu.roll(x, half, 1) * sin_hi


def _standardize(x):
    xc = x - jnp.mean(x, axis=-1, keepdims=True)
    return xc * lax.rsqrt(jnp.mean(xc * xc, axis=-1, keepdims=True) + NORM_EPS)


def _retgmlp_kernel(q_ref, k_ref, v_ref, g_ref, gu_ref, gv_ref,
                    cosq_ref, slq_ref, shq_ref, cosk_ref, slk_ref, shk_ref,
                    dmat_ref, dq_ref, dk_ref, cdec_ref, hmask_ref,
                    gnw_ref, lnw_ref, lnb_ref, ws_ref, bs_ref,
                    o_ref, sf_ref, sb_ref, sball_ref, *, n_chunks):
    phase = pl.program_id(1)
    n = pl.program_id(2)
    C = RET_CHUNK

    def head_k_t(k_scaled, h):
        p = h // 2
        return k_scaled[:, p * LANES:(p + 1) * LANES].T.astype(BF16)

    @pl.when(phase == 0)
    def _():
        @pl.when(n == 0)
        def _():
            sb_ref[...] = jnp.zeros_like(sb_ref)

        c = n_chunks - 1 - n
        k_r = _rotary(k_ref[0].astype(F32), cosk_ref[...], slk_ref[...], shk_ref[...])
        kb = k_r * dk_ref[1]
        sball_ref[c] = sb_ref[...].astype(BF16)
        for h in range(RET_HEADS):
            if h % 2 == 0:
                kbt = head_k_t(kb, h)
            v_h = v_ref[0, :, h * RET_DV:(h + 1) * RET_DV]
            sb_ref[h] = sb_ref[h] * cdec_ref[1, h:h + 1, :] + _dot(kbt, v_h)

    @pl.when(phase == 1)
    def _():
        @pl.when(n == 0)
        def _():
            sf_ref[...] = jnp.zeros_like(sf_ref)

        q_r = _rotary(q_ref[0].astype(F32), cosq_ref[...], slq_ref[...], shq_ref[...])
        k_r = _rotary(k_ref[0].astype(F32), cosk_ref[...], slk_ref[...], shk_ref[...])
        kf = k_r * dk_ref[0]
        qf = q_r * dq_ref[0]
        qb = q_r * dq_ref[1]
        k_bf = k_r.astype(BF16)
        for h in range(RET_HEADS):
            p = h // 2
            sl = slice(p * LANES, (p + 1) * LANES)
            m = hmask_ref[h % 2:h % 2 + 1, :]
            q_m = (q_r[:, sl] * m).astype(BF16)
            v_h = v_ref[0, :, h * RET_DV:(h + 1) * RET_DV]
            scores = lax.dot_general(q_m, k_bf[:, sl], (((1,), (1,)), ((), ())),
                                     preferred_element_type=F32) * dmat_ref[h]
            out = _dot(scores.astype(BF16), v_h)
            out += _dot((qf[:, sl] * m).astype(BF16), sf_ref[h].astype(BF16))
            out += _dot((qb[:, sl] * m).astype(BF16), sball_ref[n, h])
            if h % 2 == 0:
                kft = head_k_t(kf, h)
            sf_ref[h] = sf_ref[h] * cdec_ref[0, h:h + 1, :] + _dot(kft, v_h)
            vs = slice(h * RET_DV, (h + 1) * RET_DV)
            y = _standardize(out) * gnw_ref[:, vs]
            gate = g_ref[0, :, vs].astype(F32)
            o_ref[0, :, vs] = (gate * jax.nn.sigmoid(gate) * y).astype(o_ref.dtype)

        gu = jax.nn.gelu(gu_ref[0].astype(F32))
        gv = _standardize(jax.nn.gelu(gv_ref[0].astype(F32))) * lnw_ref[...] + lnb_ref[...]
        gv = gv.astype(BF16)
        for g in range(GM_GROUPS):
            gs = slice(g * GM_GROUP_DIM, (g + 1) * GM_GROUP_DIM)
            mixed = _dot(ws_ref[g], gv[:, gs]) + bs_ref[g]
            o_ref[0, :, RET_V_W + g * GM_GROUP_DIM:RET_V_W + (g + 1) * GM_GROUP_DIM] = (
                gu[:, gs] * mixed).astype(o_ref.dtype)


def _retention_tables(S, decay_logit):
    half = RET_DK // 2
    inv = ROPE_BASE ** (-jnp.arange(half, dtype=F32) / half)
    ang = jnp.arange(S, dtype=F32)[:, None] * inv[None, :]
    cos, sin = jnp.cos(ang), jnp.sin(ang)
    zeros = jnp.zeros_like(sin)
    reps = RET_QK_W // RET_DK
    cos_t = jnp.tile(jnp.concatenate([cos, cos], axis=1), (1, reps))
    sin_lo = jnp.tile(jnp.concatenate([-sin, zeros], axis=1), (1, reps))
    sin_hi = jnp.tile(jnp.concatenate([zeros, sin], axis=1), (1, reps))
    C = RET_CHUNK
    log_gamma = jax.nn.log_sigmoid(decay_logit.astype(F32))
    lf, lb = log_gamma[0], log_gamma[1]
    idx = jnp.arange(C, dtype=F32)
    diff = idx[:, None] - idx[None, :]
    adiff = jnp.abs(diff)[None]
    dmat = jnp.where((diff >= 0)[None], jnp.exp(adiff * lf[:, None, None]),
                     jnp.exp(adiff * lb[:, None, None]))

    def lanes(t):
        return jnp.repeat(t, RET_DK, axis=1)

    dq = jnp.stack([lanes(jnp.exp((idx + 1)[:, None] * lf[None, :])),
                    lanes(jnp.exp((C - 1 - idx)[:, None] * lb[None, :]))])
    dk = jnp.stack([lanes(jnp.exp((C - 1 - idx)[:, None] * lf[None, :])),
                    lanes(jnp.exp((idx + 1)[:, None] * lb[None, :]))])
    cdec = jnp.broadcast_to(jnp.exp(C * log_gamma)[:, :, None], (2, RET_HEADS, RET_DV))
    lane = jnp.arange(LANES)
    hmask = jnp.stack([(lane < RET_DK), (lane >= RET_DK)]).astype(F32) * (RET_DK ** -0.5)
    return cos_t, sin_lo, sin_hi, dmat, dq, dk, cdec, hmask


def retention_gmlp(proj, decay_logit, gn_w, ln_w, ln_b, ws, bs):
    B, S, _ = proj.shape
    C = RET_CHUNK
    N = S // C
    cos_t, sin_lo, sin_hi, dmat, dq, dk, cdec, hmask = _retention_tables(S, decay_logit)
    bs_b = jnp.broadcast_to(bs.astype(F32)[:, :, None], (GM_GROUPS, C, GM_GROUP_DIM))

    def kv_chunk(ph, n):
        return ph * n + (1 - ph) * (N - 1 - n)

    qw, vw = RET_QK_W, RET_V_W
    col = lambda width, start: start // width
    q_spec = pl.BlockSpec((1, C, qw), lambda b, ph, n: (b, ph * n, col(qw, 0)))
    k_spec = pl.BlockSpec((1, C, qw), lambda b, ph, n: (b, kv_chunk(ph, n), col(qw, qw)))
    v_spec = pl.BlockSpec((1, C, vw), lambda b, ph, n: (b, kv_chunk(ph, n), col(vw, 2 * qw)))
    g_spec = pl.BlockSpec((1, C, vw), lambda b, ph, n: (b, ph * n, col(vw, 2 * qw + vw)))
    gu_spec = pl.BlockSpec((1, C, vw), lambda b, ph, n: (b, ph * n, col(vw, 2 * qw + 2 * vw)))
    gv_spec = pl.BlockSpec((1, C, vw), lambda b, ph, n: (b, ph * n, col(vw, 2 * qw + 3 * vw)))
    tq_spec = pl.BlockSpec((C, qw), lambda b, ph, n: (ph * n, 0))
    tk_spec = pl.BlockSpec((C, qw), lambda b, ph, n: (kv_chunk(ph, n), 0))

    def const(shape):
        nd = len(shape)
        return pl.BlockSpec(shape, lambda b, ph, n: (0,) * nd)

    return pl.pallas_call(
        functools.partial(_retgmlp_kernel, n_chunks=N),
        grid=(B, 2, N),
        in_specs=[q_spec, k_spec, v_spec, g_spec, gu_spec, gv_spec,
                  tq_spec, tq_spec, tq_spec, tk_spec, tk_spec, tk_spec,
                  const(dmat.shape), const(dq.shape), const(dk.shape), const(cdec.shape), const(hmask.shape),
                  const((1, vw)), const((1, GM_W)), const((1, GM_W)), const(ws.shape), const(bs_b.shape)],
        out_specs=pl.BlockSpec((1, C, EVEN_MIX_W), lambda b, ph, n: (b, ph * n, 0)),
        out_shape=jax.ShapeDtypeStruct((B, S, EVEN_MIX_W), BF16),
        scratch_shapes=[pltpu.VMEM((RET_HEADS, LANES, RET_DV), F32),
                        pltpu.VMEM((RET_HEADS, LANES, RET_DV), F32),
                        pltpu.VMEM((N, RET_HEADS, LANES, RET_DV), BF16)],
        compiler_params=_cparams(("parallel", "arbitrary", "arbitrary")),
        name="retention_gmlp",
    )(proj, proj, proj, proj, proj, proj, cos_t, sin_lo, sin_hi, cos_t, sin_lo, sin_hi,
      dmat, dq, dk, cdec, hmask, gn_w.reshape(1, vw).astype(F32), ln_w.reshape(1, GM_W).astype(F32),
      ln_b.reshape(1, GM_W).astype(F32), ws.astype(BF16), bs_b)


def _conv_kernel(xm_ref, xp_ref, xn_ref, cw_ref, cs_ref, o_ref, pad_ref, *, ts, l2):
    i = pl.program_id(1)
    last = pl.num_programs(1) - 1
    pad_ref[0:HALO, :] = jnp.where(i > 0, xp_ref[0].astype(F32), 0.0)
    pad_ref[HALO:HALO + ts, :] = xm_ref[0].astype(F32)
    pad_ref[HALO + ts:2 * HALO + ts, :] = jnp.where(i < last, xn_ref[0].astype(F32), 0.0)
    base = HALO - GDN_CONV // 2
    acc = cw_ref[0:1, :] * pad_ref[base:base + ts, :]
    for w in range(1, GDN_CONV):
        acc = acc + cw_ref[w:w + 1, :] * pad_ref[base + w:base + w + ts, :]
    y = acc * jax.nn.sigmoid(acc)
    if l2:
        tc = y.shape[1]
        for hd in range(tc // GDN_DK):
            hs = slice(hd * GDN_DK, (hd + 1) * GDN_DK)
            yh = y[:, hs]
            yh = yh * lax.rsqrt(jnp.sum(yh * yh, axis=-1, keepdims=True) + L2_EPS) * cs_ref[:, hs]
            o_ref[0, :, hs] = yh.astype(o_ref.dtype)
    else:
        o_ref[0] = y.astype(o_ref.dtype)


def conv_silu(proj, conv_w, colscale, col0, width, l2, ts, tc):
    B, S, _ = proj.shape
    cb0 = col0 // tc
    nt = S // ts
    hb = ts // HALO
    return pl.pallas_call(
        functools.partial(_conv_kernel, ts=ts, l2=l2),
        grid=(B, nt, width // tc),
        in_specs=[pl.BlockSpec((1, ts, tc), lambda b, i, j: (b, i, cb0 + j)),
                  pl.BlockSpec((1, HALO, tc), lambda b, i, j: (b, jnp.maximum(i * hb - 1, 0), cb0 + j)),
                  pl.BlockSpec((1, HALO, tc), lambda b, i, j: (b, jnp.minimum((i + 1) * hb, S // HALO - 1), cb0 + j)),
                  pl.BlockSpec((GDN_CONV, tc), lambda b, i, j: (0, cb0 + j)),
                  pl.BlockSpec((1, tc), lambda b, i, j: (0, j))],
        out_specs=pl.BlockSpec((1, ts, tc), lambda b, i, j: (b, i, j)),
        out_shape=jax.ShapeDtypeStruct((B, S, width), BF16),
        scratch_shapes=[pltpu.VMEM((ts + 2 * HALO, tc), F32)],
        compiler_params=_cparams(("parallel", "arbitrary", "arbitrary")),
        name="conv_silu_l2" if l2 else "conv_silu",
    )(proj, proj, proj, conv_w, colscale)


def _split3(x):
    hi = x.astype(BF16)
    r = x - hi.astype(F32)
    mid = r.astype(BF16)
    lo = (r - mid.astype(F32)).astype(BF16)
    return hi, mid, lo


def _gates_kernel(x_ref, dtb_ref, rate_ref, o_ref, *, ts):
    G = GDN_BLOCK
    row = lax.broadcasted_iota(jnp.int32, (G, G), 0)
    col = lax.broadcasted_iota(jnp.int32, (G, G), 1)
    lower = (row >= col).astype(BF16)
    upper = (row <= col).astype(BF16)
    lane = lax.broadcasted_iota(jnp.int32, (G, GDN_GATE_W), 1)
    for c in range(ts // G):
        x = x_ref[c * G:(c + 1) * G, :]
        beta = jax.nn.sigmoid(x)
        z = x + dtb_ref[...]
        softplus = jnp.maximum(z, 0.0) + jnp.log1p(jnp.exp(-jnp.abs(z)))
        g = -rate_ref[...] * softplus
        hi, mid, lo = _split3(g)
        prefix = _dot(lower, hi) + _dot(lower, mid) + _dot(lower, lo)
        suffix = _dot(upper, hi) + _dot(upper, mid) + _dot(upper, lo)
        out = jnp.where(lane < 2 * GDN_V_HEADS, beta, jnp.where(lane < 3 * GDN_V_HEADS, prefix, suffix))
        o_ref[0, :, c * G:(c + 1) * G] = out.T


def gdn_gates(gproj, a_log, dt_bias, B, S, ts):
    zeros = jnp.zeros((2 * GDN_V_HEADS,), F32)
    dtb = jnp.concatenate([zeros, dt_bias.astype(F32).reshape(-1)]).reshape(1, GDN_GATE_W)
    rate = jnp.concatenate([zeros, jnp.exp(a_log.astype(F32)).reshape(-1)]).reshape(1, GDN_GATE_W)
    nt = S // ts
    return pl.pallas_call(
        functools.partial(_gates_kernel, ts=ts),
        grid=(B, nt),
        in_specs=[pl.BlockSpec((ts, GDN_GATE_W), lambda b, i: (b * nt + i, 0)),
                  pl.BlockSpec((1, GDN_GATE_W), lambda b, i: (0, 0)),
                  pl.BlockSpec((1, GDN_GATE_W), lambda b, i: (0, 0))],
        out_specs=pl.BlockSpec((1, GDN_GATE_W, ts), lambda b, i: (b, 0, i)),
        out_shape=jax.ShapeDtypeStruct((B, GDN_GATE_W, S), F32),
        compiler_params=_cparams(("parallel", "arbitrary")),
        name="gdn_gates",
    )(gproj, dtb, rate)


def _take_rows(x, m, second):
    G = x.shape[0]
    off = m if second else 0
    return jnp.concatenate([x[g * 2 * m + off:g * 2 * m + off + m] for g in range(G // (2 * m))], axis=0)


def _put_rows(base, upd, m, second):
    G = 2 * upd.shape[0]
    parts = []
    for g in range(G // (2 * m)):
        piece = upd[g * m:(g + 1) * m]
        lo = g * 2 * m
        if base is None:
            other = jnp.zeros_like(piece)
        else:
            other = base[lo:lo + m] if second else base[lo + m:lo + 2 * m]
        parts += [other, piece] if second else [piece, other]
    return jnp.concatenate(parts, axis=0)


def _unit_triangular_inverses(problems, row, col):
    G = problems[0][0].shape[0]

    def siblings(k):
        return ((row >> (k + 1)) == (col >> (k + 1))) & ((row >> k) != (col >> k))

    eye = (row == col).astype(F32)
    sib = siblings(0)
    tds = [eye - jnp.where(sib, a, 0.0) for a, _ in problems]
    for k in range(1, int(math.log2(G))):
        m = 1 << k
        sib = siblings(k)
        part = m >= 8
        xs = []
        for (a, upper), td in zip(problems, tds):
            a_off = jnp.where(sib, a, 0.0)
            if part:
                a_off = _take_rows(a_off, m, not upper)
            xs.append(_dot(a_off.astype(BF16), td.astype(BF16)))
        new = []
        for (a, upper), td, x in zip(problems, tds, xs):
            if part:
                x_full = _put_rows(None, x, m, not upper).astype(BF16)
                td_nz = _take_rows(td, m, not upper)
                new.append(_put_rows(td, td_nz - _dot(td_nz.astype(BF16), x_full), m, not upper))
            else:
                new.append(td - _dot(td.astype(BF16), x.astype(BF16)))
        tds = new
    return tds


def _gdn_kernel(q_ref, k_ref, kt_ref, v_ref, z_ref, g_ref, nw_ref, o_ref,
                u_s, w_s, qg_s, qkd_s, kdt_s, eg_s, st_s, o_s, *, n_blocks):
    G = GDN_BLOCK
    N = n_blocks
    row = lax.broadcasted_iota(jnp.int32, (G, G), 0)
    col = lax.broadcasted_iota(jnp.int32, (G, G), 1)
    halves = (slice(0, G), slice(G, 2 * G))
    masks = ((row >= col, row > col), (row <= col, row < col))

    def block_tokens(step, d):
        c = step if d == 0 else N - 1 - step
        return pl.ds(pl.multiple_of(c * G, G), G)

    def scan_stage1(step, slot):
        out = []
        for d in range(2):
            s = st_s[d]
            lhs = jnp.concatenate([w_s[slot, d], qg_s[slot, d]], axis=0)
            r = jnp.concatenate([_dot(lhs[:, hs], s[:, hs].astype(BF16)) for hs in halves], axis=1)
            out.append((s, r))
        return out

    def scan_stage2(step, slot, completes, stage1):
        for d, (s, r) in enumerate(stage1):
            tok = block_tokens(step, d)
            v_new = (u_s[slot, d] - r[:G]).astype(BF16)
            lhs = jnp.concatenate([qkd_s[slot, d], kdt_s[slot, d]], axis=0)
            r2 = jnp.concatenate([_dot(lhs[:, hs], v_new[:, hs]) for hs in halves], axis=1)
            st_s[d] = s * eg_s[slot, d, 0:1, :] + r2[G:]
            o = r[G:] + r2[:G]
            if completes:
                o = o + o_s[tok, :]
                z = z_ref[0, tok, :].astype(F32)
                for hs in halves:
                    zh = z[:, hs]
                    o_ref[0, tok, hs] = (_rms(o[:, hs], nw_ref[...]) * (zh * jax.nn.sigmoid(zh))).astype(o_ref.dtype)
            else:
                o_s[tok, :] = o

    def two_steps(scans, prepares, completes):
        jobs = [(step, slot, d) for step, slot in prepares for d in range(2)]
        loaded = []
        for step, slot, d in jobs:
            tok = block_tokens(step, d)
            qc, kc, ktc = q_ref[0, tok, :], k_ref[0, tok, :], kt_ref[0, 0, :, tok]
            loaded.append((tok, qc, kc, ktc, _dot(kc, ktc), _dot(qc, ktc)))
        if scans:
            stage1 = scan_stage1(*scans[0])
        mids, problems = [], []
        for (step, slot, d), (tok, qc, kc, ktc, kk, qk) in zip(jobs, loaded):
            tri, strict = masks[d]
            beta_r = [g_ref[0, hd, d:d + 1, tok] for hd in range(2)]
            gc_r = [g_ref[0, hd, 2 + d:3 + d, tok] for hd in range(2)]
            per_head = []
            for hd in range(2):
                beta_c = jnp.broadcast_to(beta_r[hd], (G, G)).T
                gc_c = jnp.broadcast_to(gc_r[hd], (G, G)).T
                decay = jnp.where(tri, jnp.exp(jnp.where(tri, gc_c - gc_r[hd], 0.0)), 0.0)
                problems.append((jnp.where(strict, beta_c * kk * decay, 0.0), d == 1))
                per_head.append((beta_c, gc_c, decay))
            mids.append((beta_r, gc_r, per_head))
        if scans:
            scan_stage2(*scans[0], completes, stage1)
            stage1 = scan_stage1(*scans[1])
        t_invs = _unit_triangular_inverses(problems, row, col)
        if scans:
            scan_stage2(*scans[1], completes, stage1)
        for j, ((step, slot, d), (tok, qc, kc, ktc, kk, qk), (beta_r, gc_r, per_head)) in enumerate(
                zip(jobs, loaded, mids)):
            vc = v_ref[0, tok, :].astype(F32)
            qf, kf, ktf = qc.astype(F32), kc.astype(F32), ktc.astype(F32)
            for hd, hs in enumerate(halves):
                beta_c, gc_c, decay = per_head[hd]
                e_c = jnp.exp(gc_c)
                rhs = jnp.concatenate([(beta_c * vc[:, hs]).astype(BF16), (beta_c * e_c * kf).astype(BF16)], axis=1)
                uw = _dot(t_invs[2 * j + hd].astype(BF16), rhs)
                u_s[slot, d, :, hs] = uw[:, :G]
                w_s[slot, d, :, hs] = uw[:, G:].astype(BF16)
                qg_s[slot, d, :, hs] = (qf * e_c).astype(BF16)
                qkd_s[slot, d, :, hs] = (qk * decay).astype(BF16)
                g_last = gc_r[hd][:, G - 1:G] if d == 0 else gc_r[hd][:, 0:1]
                kdt_s[slot, d, :, hs] = (ktf * jnp.exp(g_last - gc_r[hd])).astype(BF16)
                eg_s[slot, d, :, hs] = jnp.broadcast_to(jnp.exp(g_last), (8, G))

    st_s[...] = jnp.zeros_like(st_s)
    two_steps(None, ((0, 0), (1, 1)), False)

    def four_steps(i, carry, completes):
        n = 4 * i
        last = N - 1
        two_steps(((n, 0), (n + 1, 1)), ((n + 2, 2), (n + 3, 3)), completes)
        two_steps(((n + 2, 2), (n + 3, 3)),
                  ((jnp.minimum(n + 4, last), 0), (jnp.minimum(n + 5, last), 1)), completes)
        return carry

    lax.fori_loop(0, N // 8, functools.partial(four_steps, completes=False), 0)
    lax.fori_loop(N // 8, N // 4, functools.partial(four_steps, completes=True), 0)


def gdn_scan(qk, kt, v, proj, gates, norm_w):
    B, S, _ = v.shape
    G = GDN_BLOCK
    nb = S // G
    assert nb % 8 == 0 and GDN_V_HEADS == 2 * GDN_K_HEADS and GDN_DK == G and GDN_DV == G
    zb0 = GDN_CONV_W // (2 * GDN_DV)
    narrow = lambda f: pl.BlockSpec((1, S, G), f)
    wide = lambda f: pl.BlockSpec((1, S, 2 * G), f)
    slots = lambda rows, dtype: pltpu.VMEM((4, 2, rows, 2 * G), dtype)
    return pl.pallas_call(
        functools.partial(_gdn_kernel, n_blocks=nb),
        grid=(B, GDN_K_HEADS),
        in_specs=[narrow(lambda b, h: (b, 0, h)),
                  narrow(lambda b, h: (b, 0, GDN_K_HEADS + h)),
                  pl.BlockSpec((1, 1, GDN_DK, S), lambda b, h: (b, h, 0, 0)),
                  wide(lambda b, h: (b, 0, h)),
                  wide(lambda b, h: (b, 0, zb0 + h)),
                  pl.BlockSpec((1, 2, 4, S), lambda b, h: (b, h, 0, 0)),
                  pl.BlockSpec((1, GDN_DV), lambda b, h: (0, 0))],
        out_specs=wide(lambda b, h: (b, 0, h)),
        out_shape=jax.ShapeDtypeStruct((B, S, GDN_V_W), BF16),
        scratch_shapes=[slots(G, F32),
                        slots(G, BF16),
                        slots(G, BF16),
                        slots(G, BF16),
                        slots(G, BF16),
                        slots(8, F32),
                        pltpu.VMEM((2, G, 2 * G), F32),
                        pltpu.VMEM((S, 2 * G), F32)],
        compiler_params=_cparams(("parallel", "arbitrary")),
        name="gdn_scan",
    )(qk, qk, kt, v, proj, gates, norm_w.reshape(1, GDN_DV).astype(F32))


def _tile(total, want):
    t = min(want, total)
    while total % t:
        t //= 2
    return t


def _encoder(x, p):
    B, S, D = x.shape
    T = B * S
    h = x.reshape(T, D)
    tm = _tile(T, 1024)

    proj = norm_matmul(h, p["norm1_w"][0], p["ev_w_in"], BF16, tm, 512)
    mix = retention_gmlp(proj.reshape(B, S, EVEN_IN_W), p["ev_ret_decay_logit"], p["ev_ret_gn_w"],
                         p["ev_gm_ln_w"], p["ev_gm_ln_b"], p["ev_gm_ws"], p["ev_gm_bs"])
    h = matmul_residual(mix.reshape(T, EVEN_MIX_W), p["ev_w_out"], h, tm, 512)
    h = ffn(h, p["norm2_w"][0], p["ffn_w_gate"][0], p["ffn_w_up"][0], p["ffn_w_down"][0],
            p["final_norm_w"], False, _tile(T, 512), 512)

    proj = norm_matmul(h, p["norm1_w"][1], p["od_w_main"], BF16, tm, 512).reshape(B, S, GDN_MAIN_W)
    gproj = norm_matmul(h, p["norm1_w"][1], p["od_w_gate"], F32, tm, GDN_GATE_W)
    ts = _tile(S, 512)
    qk = conv_silu(proj, p["od_conv_w"], p["od_qk_scale"], 0, 2 * GDN_QK_W, True, ts, 512)
    v = conv_silu(proj, p["od_conv_w"], p["od_qk_scale"], 2 * GDN_QK_W, GDN_V_W, False, ts, 512)
    gates = gdn_gates(gproj, p["od_a_log"], p["od_dt_bias"], B, S, ts)
    gates = gates.reshape(B, 4, GDN_V_HEADS, S).transpose(0, 2, 1, 3)
    kt = qk[:, :, GDN_QK_W:].reshape(B, S, GDN_K_HEADS, GDN_DK).transpose(0, 2, 3, 1)
    o = gdn_scan(qk, kt, v, proj, gates, p["od_norm_w"])
    h = matmul_residual(o.reshape(T, GDN_V_W), p["od_w_out"], h, tm, 512)
    h = ffn(h, p["norm2_w"][1], p["ffn_w_gate"][1], p["ffn_w_up"][1], p["ffn_w_down"][1],
            p["final_norm_w"], True, _tile(T, 512), 512)
    return h.reshape(B, S, D)


def kernel(x_prompt, x_sample, norm1_w, norm2_w, final_norm_w, ev_w_in, ev_w_out, ev_ret_decay_logit, ev_ret_gn_w, ev_gm_ln_w, ev_gm_ln_b, ev_gm_ws, ev_gm_bs, od_w_in, od_conv_w, od_a_log, od_dt_bias, od_norm_w, od_w_out, ffn_w_gate, ffn_w_up, ffn_w_down):
    qk_scale = jnp.concatenate([jnp.full((GDN_QK_W,), GDN_DK ** -0.5, F32), jnp.ones((GDN_QK_W,), F32)])
    p = {
        "norm1_w": norm1_w.astype(F32), "norm2_w": norm2_w.astype(F32), "final_norm_w": final_norm_w.astype(F32),
        "ev_w_in": ev_w_in[0].astype(BF16), "ev_w_out": ev_w_out[0].astype(BF16),
        "ev_ret_decay_logit": ev_ret_decay_logit[0], "ev_ret_gn_w": ev_ret_gn_w[0],
        "ev_gm_ln_w": ev_gm_ln_w[0], "ev_gm_ln_b": ev_gm_ln_b[0], "ev_gm_ws": ev_gm_ws[0], "ev_gm_bs": ev_gm_bs[0],
        "od_w_main": od_w_in[0, :, :GDN_MAIN_W].astype(BF16), "od_w_gate": od_w_in[0, :, GDN_MAIN_W:].astype(BF16),
        "od_conv_w": od_conv_w[0].astype(F32), "od_qk_scale": qk_scale.reshape(1, -1),
        "od_a_log": od_a_log[0], "od_dt_bias": od_dt_bias[0], "od_norm_w": od_norm_w[0],
        "od_w_out": od_w_out[0].astype(BF16),
        "ffn_w_gate": ffn_w_gate.astype(BF16), "ffn_w_up": ffn_w_up.astype(BF16), "ffn_w_down": ffn_w_down.astype(BF16),
    }
    return (_encoder(x_prompt, p), _encoder(x_sample, p))
```

```python
import functools
import math

import jax
import jax.numpy as jnp
from jax import lax
from jax.experimental import pallas as pl
from jax.experimental.pallas import tpu as pltpu

F32 = jnp.float32
BF16 = jnp.bfloat16

D_MODEL = 2048
NORM_EPS = 1e-6
L2_EPS = 1e-6
RET_HEADS = 8
RET_DK = 64
RET_DV = 128
RET_CHUNK = 128
ROPE_BASE = 10000.0
RET_QK_W = RET_HEADS * RET_DK
RET_V_W = RET_HEADS * RET_DV
GM_GROUPS = 8
GM_GROUP_DIM = 128
GM_W = GM_GROUPS * GM_GROUP_DIM
EVEN_IN_W = 2 * RET_QK_W + 2 * RET_V_W + 2 * GM_W
EVEN_MIX_W = RET_V_W + GM_W
GDN_K_HEADS = 16
GDN_V_HEADS = 32
GDN_DK = 128
GDN_DV = 128
GDN_CONV = 5
GDN_QK_W = GDN_K_HEADS * GDN_DK
GDN_V_W = GDN_V_HEADS * GDN_DV
GDN_CONV_W = 2 * GDN_QK_W + GDN_V_W
GDN_MAIN_W = GDN_CONV_W + GDN_V_W
GDN_GATE_W = 4 * GDN_V_HEADS
GDN_BLOCK = 128
GDN_GROUP = 4
LANES = 128
HALO = 16
VMEM_LIMIT = 56 * 1024 * 1024


def _cparams(sem):
    return pltpu.CompilerParams(dimension_semantics=sem, vmem_limit_bytes=VMEM_LIMIT)


def _rms(x, w):
    return x * lax.rsqrt(jnp.mean(x * x, axis=-1, keepdims=True) + NORM_EPS) * w


def _dot(a, b):
    return jnp.dot(a, b, preferred_element_type=F32)


def _norm_matmul_kernel(x_ref, nw_ref, w_ref, o_ref, hn_ref):
    @pl.when(pl.program_id(1) == 0)
    def _():
        hn_ref[...] = _rms(x_ref[...], nw_ref[...]).astype(BF16)

    o_ref[...] = _dot(hn_ref[...], w_ref[...]).astype(o_ref.dtype)


def norm_matmul(x, nw, w, out_dtype, tm, tn):
    T, D = x.shape
    N = w.shape[1]
    return pl.pallas_call(
        _norm_matmul_kernel,
        grid=(T // tm, N // tn),
        in_specs=[pl.BlockSpec((tm, D), lambda i, j: (i, 0)),
                  pl.BlockSpec((1, D), lambda i, j: (0, 0)),
                  pl.BlockSpec((D, tn), lambda i, j: (0, j))],
        out_specs=pl.BlockSpec((tm, tn), lambda i, j: (i, j)),
        out_shape=jax.ShapeDtypeStruct((T, N), out_dtype),
        scratch_shapes=[pltpu.VMEM((tm, D), BF16)],
        compiler_params=_cparams(("parallel", "arbitrary")),
        name="norm_matmul",
    )(x, nw.reshape(1, D), w)


def _matmul_residual_kernel(a_ref, w_ref, h_ref, o_ref):
    o_ref[...] = h_ref[...] + _dot(a_ref[...], w_ref[...])


def matmul_residual(a, w, h, tm, tn):
    T, K = a.shape
    N = w.shape[1]
    return pl.pallas_call(
        _matmul_residual_kernel,
        grid=(T // tm, N // tn),
        in_specs=[pl.BlockSpec((tm, K), lambda i, j: (i, 0)),
                  pl.BlockSpec((K, tn), lambda i, j: (0, j)),
                  pl.BlockSpec((tm, tn), lambda i, j: (i, j))],
        out_specs=pl.BlockSpec((tm, tn), lambda i, j: (i, j)),
        out_shape=jax.ShapeDtypeStruct((T, N), F32),
        compiler_params=_cparams(("parallel", "arbitrary")),
        name="matmul_residual",
    )(a, w, h)


def _ffn_kernel(h_ref, nw_ref, wg_ref, wu_ref, wd_ref, fw_ref, o_ref, hn_ref, *, final_norm):
    f = pl.program_id(1)

    @pl.when(f == 0)
    def _():
        h = h_ref[...]
        hn_ref[...] = _rms(h, nw_ref[...]).astype(BF16)
        o_ref[...] = h

    hn = hn_ref[...]
    gate = _dot(hn, wg_ref[...])
    up = _dot(hn, wu_ref[...])
    act = (gate * jax.nn.sigmoid(gate) * up).astype(BF16)
    o_ref[...] += _dot(act, wd_ref[...])

    if final_norm:
        @pl.when(f == pl.num_programs(1) - 1)
        def _():
            o_ref[...] = _rms(o_ref[...], fw_ref[...])


def ffn(h, nw, wg, wu, wd, fw, final_norm, tm, tf):
    T, D = h.shape
    F = wg.shape[1]
    return pl.pallas_call(
        functools.partial(_ffn_kernel, final_norm=final_norm),
        grid=(T // tm, F // tf),
        in_specs=[pl.BlockSpec((tm, D), lambda i, f: (i, 0)),
                  pl.BlockSpec((1, D), lambda i, f: (0, 0)),
                  pl.BlockSpec((D, tf), lambda i, f: (0, f)),
                  pl.BlockSpec((D, tf), lambda i, f: (0, f)),
                  pl.BlockSpec((tf, D), lambda i, f: (f, 0)),
                  pl.BlockSpec((1, D), lambda i, f: (0, 0))],
        out_specs=pl.BlockSpec((tm, D), lambda i, f: (i, 0)),
        out_shape=jax.ShapeDtypeStruct((T, D), F32),
        scratch_shapes=[pltpu.VMEM((tm, D), BF16)],
        compiler_params=_cparams(("parallel", "arbitrary")),
        name="ffn",
    )(h, nw.reshape(1, D), wg, wu, wd, fw.reshape(1, D))


def _rotary(x, cos, sin_lo, sin_hi):
    n = x.shape[-1]
    half = RET_DK // 2
    return x * cos + pltpu.roll(x, n - half, 1) * sin_lo + pltpu.roll(x, half, 1) * sin_hi


def _standardize(x):
    xc = x - jnp.mean(x, axis=-1, keepdims=True)
    return xc * lax.rsqrt(jnp.mean(xc * xc, axis=-1, keepdims=True) + NORM_EPS)


def _retgmlp_kernel(q_ref, k_ref, v_ref, g_ref, gu_ref, gv_ref,
                    cosq_ref, slq_ref, shq_ref, cosk_ref, slk_ref, shk_ref,
                    dmat_ref, dq_ref, dk_ref, cdec_ref, hmask_ref,
                    gnw_ref, lnw_ref, lnb_ref, ws_ref, bs_ref,
                    o_ref, sf_ref, sb_ref, sball_ref, *, n_chunks):
    phase = pl.program_id(1)
    n = pl.program_id(2)
    C = RET_CHUNK

    def head_k_t(k_scaled, h):
        p = h // 2
        return k_scaled[:, p * LANES:(p + 1) * LANES].T.astype(BF16)

    @pl.when(phase == 0)
    def _():
        @pl.when(n == 0)
        def _():
            sb_ref[...] = jnp.zeros_like(sb_ref)

        c = n_chunks - 1 - n
        k_r = _rotary(k_ref[0].astype(F32), cosk_ref[...], slk_ref[...], shk_ref[...])
        kb = k_r * dk_ref[1]
        sball_ref[c] = sb_ref[...].astype(BF16)
        for h in range(RET_HEADS):
            if h % 2 == 0:
                kbt = head_k_t(kb, h)
            v_h = v_ref[0, :, h * RET_DV:(h + 1) * RET_DV]
            sb_ref[h] = sb_ref[h] * cdec_ref[1, h:h + 1, :] + _dot(kbt, v_h)

    @pl.when(phase == 1)
    def _():
        @pl.when(n == 0)
        def _():
            sf_ref[...] = jnp.zeros_like(sf_ref)

        q_r = _rotary(q_ref[0].astype(F32), cosq_ref[...], slq_ref[...], shq_ref[...])
        k_r = _rotary(k_ref[0].astype(F32), cosk_ref[...], slk_ref[...], shk_ref[...])
        kf = k_r * dk_ref[0]
        qf = q_r * dq_ref[0]
        qb = q_r * dq_ref[1]
        k_bf = k_r.astype(BF16)
        for h in range(RET_HEADS):
            p = h // 2
            sl = slice(p * LANES, (p + 1) * LANES)
            m = hmask_ref[h % 2:h % 2 + 1, :]
            q_m = (q_r[:, sl] * m).astype(BF16)
            v_h = v_ref[0, :, h * RET_DV:(h + 1) * RET_DV]
            scores = lax.dot_general(q_m, k_bf[:, sl], (((1,), (1,)), ((), ())),
                                     preferred_element_type=F32) * dmat_ref[h]
            out = _dot(scores.astype(BF16), v_h)
            out += _dot((qf[:, sl] * m).astype(BF16), sf_ref[h].astype(BF16))
            out += _dot((qb[:, sl] * m).astype(BF16), sball_ref[n, h])
            if h % 2 == 0:
                kft = head_k_t(kf, h)
            sf_ref[h] = sf_ref[h] * cdec_ref[0, h:h + 1, :] + _dot(kft, v_h)
            vs = slice(h * RET_DV, (h + 1) * RET_DV)
            y = _standardize(out) * gnw_ref[:, vs]
            gate = g_ref[0, :, vs].astype(F32)
            o_ref[0, :, vs] = (gate * jax.nn.sigmoid(gate) * y).astype(o_ref.dtype)

        gu = jax.nn.gelu(gu_ref[0].astype(F32))
        gv = _standardize(jax.nn.gelu(gv_ref[0].astype(F32))) * lnw_ref[...] + lnb_ref[...]
        gv = gv.astype(BF16)
        for g in range(GM_GROUPS):
            gs = slice(g * GM_GROUP_DIM, (g + 1) * GM_GROUP_DIM)
            mixed = _dot(ws_ref[g], gv[:, gs]) + bs_ref[g]
            o_ref[0, :, RET_V_W + g * GM_GROUP_DIM:RET_V_W + (g + 1) * GM_GROUP_DIM] = (
                gu[:, gs] * mixed).astype(o_ref.dtype)


def _retention_tables(S, decay_logit):
    half = RET_DK // 2
    inv = ROPE_BASE ** (-jnp.arange(half, dtype=F32) / half)
    ang = jnp.arange(S, dtype=F32)[:, None] * inv[None, :]
    cos, sin = jnp.cos(ang), jnp.sin(ang)
    zeros = jnp.zeros_like(sin)
    reps = RET_QK_W // RET_DK
    cos_t = jnp.tile(jnp.concatenate([cos, cos], axis=1), (1, reps))
    sin_lo = jnp.tile(jnp.concatenate([-sin, zeros], axis=1), (1, reps))
    sin_hi = jnp.tile(jnp.concatenate([zeros, sin], axis=1), (1, reps))
    C = RET_CHUNK
    log_gamma = jax.nn.log_sigmoid(decay_logit.astype(F32))
    lf, lb = log_gamma[0], log_gamma[1]
    idx = jnp.arange(C, dtype=F32)
    diff = idx[:, None] - idx[None, :]
    adiff = jnp.abs(diff)[None]
    dmat = jnp.where((diff >= 0)[None], jnp.exp(adiff * lf[:, None, None]),
                     jnp.exp(adiff * lb[:, None, None]))

    def lanes(t):
        return jnp.repeat(t, RET_DK, axis=1)

    dq = jnp.stack([lanes(jnp.exp((idx + 1)[:, None] * lf[None, :])),
                    lanes(jnp.exp((C - 1 - idx)[:, None] * lb[None, :]))])
    dk = jnp.stack([lanes(jnp.exp((C - 1 - idx)[:, None] * lf[None, :])),
                    lanes(jnp.exp((idx + 1)[:, None] * lb[None, :]))])
    cdec = jnp.broadcast_to(jnp.exp(C * log_gamma)[:, :, None], (2, RET_HEADS, RET_DV))
    lane = jnp.arange(LANES)
    hmask = jnp.stack([(lane < RET_DK), (lane >= RET_DK)]).astype(F32) * (RET_DK ** -0.5)
    return cos_t, sin_lo, sin_hi, dmat, dq, dk, cdec, hmask


def retention_gmlp(proj, decay_logit, gn_w, ln_w, ln_b, ws, bs):
    B, S, _ = proj.shape
    C = RET_CHUNK
    N = S // C
    cos_t, sin_lo, sin_hi, dmat, dq, dk, cdec, hmask = _retention_tables(S, decay_logit)
    bs_b = jnp.broadcast_to(bs.astype(F32)[:, :, None], (GM_GROUPS, C, GM_GROUP_DIM))

    def kv_chunk(ph, n):
        return ph * n + (1 - ph) * (N - 1 - n)

    qw, vw = RET_QK_W, RET_V_W
    col = lambda width, start: start // width
    q_spec = pl.BlockSpec((1, C, qw), lambda b, ph, n: (b, ph * n, col(qw, 0)))
    k_spec = pl.BlockSpec((1, C, qw), lambda b, ph, n: (b, kv_chunk(ph, n), col(qw, qw)))
    v_spec = pl.BlockSpec((1, C, vw), lambda b, ph, n: (b, kv_chunk(ph, n), col(vw, 2 * qw)))
    g_spec = pl.BlockSpec((1, C, vw), lambda b, ph, n: (b, ph * n, col(vw, 2 * qw + vw)))
    gu_spec = pl.BlockSpec((1, C, vw), lambda b, ph, n: (b, ph * n, col(vw, 2 * qw + 2 * vw)))
    gv_spec = pl.BlockSpec((1, C, vw), lambda b, ph, n: (b, ph * n, col(vw, 2 * qw + 3 * vw)))
    tq_spec = pl.BlockSpec((C, qw), lambda b, ph, n: (ph * n, 0))
    tk_spec = pl.BlockSpec((C, qw), lambda b, ph, n: (kv_chunk(ph, n), 0))

    def const(shape):
        nd = len(shape)
        return pl.BlockSpec(shape, lambda b, ph, n: (0,) * nd)

    return pl.pallas_call(
        functools.partial(_retgmlp_kernel, n_chunks=N),
        grid=(B, 2, N),
        in_specs=[q_spec, k_spec, v_spec, g_spec, gu_spec, gv_spec,
                  tq_spec, tq_spec, tq_spec, tk_spec, tk_spec, tk_spec,
                  const(dmat.shape), const(dq.shape), const(dk.shape), const(cdec.shape), const(hmask.shape),
                  const((1, vw)), const((1, GM_W)), const((1, GM_W)), const(ws.shape), const(bs_b.shape)],
        out_specs=pl.BlockSpec((1, C, EVEN_MIX_W), lambda b, ph, n: (b, ph * n, 0)),
        out_shape=jax.ShapeDtypeStruct((B, S, EVEN_MIX_W), BF16),
        scratch_shapes=[pltpu.VMEM((RET_HEADS, LANES, RET_DV), F32),
                        pltpu.VMEM((RET_HEADS, LANES, RET_DV), F32),
                        pltpu.VMEM((N, RET_HEADS, LANES, RET_DV), BF16)],
        compiler_params=_cparams(("parallel", "arbitrary", "arbitrary")),
        name="retention_gmlp",
    )(proj, proj, proj, proj, proj, proj, cos_t, sin_lo, sin_hi, cos_t, sin_lo, sin_hi,
      dmat, dq, dk, cdec, hmask, gn_w.reshape(1, vw).astype(F32), ln_w.reshape(1, GM_W).astype(F32),
      ln_b.reshape(1, GM_W).astype(F32), ws.astype(BF16), bs_b)


def _conv_kernel(xm_ref, xp_ref, xn_ref, cw_ref, cs_ref, o_ref, pad_ref, *, ts, l2):
    i = pl.program_id(1)
    last = pl.num_programs(1) - 1
    pad_ref[0:HALO, :] = jnp.where(i > 0, xp_ref[0].astype(F32), 0.0)
    pad_ref[HALO:HALO + ts, :] = xm_ref[0].astype(F32)
    pad_ref[HALO + ts:2 * HALO + ts, :] = jnp.where(i < last, xn_ref[0].astype(F32), 0.0)
    base = HALO - GDN_CONV // 2
    acc = cw_ref[0:1, :] * pad_ref[base:base + ts, :]
    for w in range(1, GDN_CONV):
        acc = acc + cw_ref[w:w + 1, :] * pad_ref[base + w:base + w + ts, :]
    y = acc * jax.nn.sigmoid(acc)
    if l2:
        tc = y.shape[1]
        for hd in range(tc // GDN_DK):
            hs = slice(hd * GDN_DK, (hd + 1) * GDN_DK)
            yh = y[:, hs]
            yh = yh * lax.rsqrt(jnp.sum(yh * yh, axis=-1, keepdims=True) + L2_EPS) * cs_ref[:, hs]
            o_ref[0, :, hs] = yh.astype(o_ref.dtype)
    else:
        o_ref[0] = y.astype(o_ref.dtype)


def conv_silu(proj, conv_w, colscale, col0, width, l2, ts, tc):
    B, S, _ = proj.shape
    cb0 = col0 // tc
    nt = S // ts
    hb = ts // HALO
    return pl.pallas_call(
        functools.partial(_conv_kernel, ts=ts, l2=l2),
        grid=(B, nt, width // tc),
        in_specs=[pl.BlockSpec((1, ts, tc), lambda b, i, j: (b, i, cb0 + j)),
                  pl.BlockSpec((1, HALO, tc), lambda b, i, j: (b, jnp.maximum(i * hb - 1, 0), cb0 + j)),
                  pl.BlockSpec((1, HALO, tc), lambda b, i, j: (b, jnp.minimum((i + 1) * hb, S // HALO - 1), cb0 + j)),
                  pl.BlockSpec((GDN_CONV, tc), lambda b, i, j: (0, cb0 + j)),
                  pl.BlockSpec((1, tc), lambda b, i, j: (0, j))],
        out_specs=pl.BlockSpec((1, ts, tc), lambda b, i, j: (b, i, j)),
        out_shape=jax.ShapeDtypeStruct((B, S, width), BF16),
        scratch_shapes=[pltpu.VMEM((ts + 2 * HALO, tc), F32)],
        compiler_params=_cparams(("parallel", "arbitrary", "arbitrary")),
        name="conv_silu_l2" if l2 else "conv_silu",
    )(proj, proj, proj, conv_w, colscale)


def _split3(x):
    hi = x.astype(BF16)
    r = x - hi.astype(F32)
    mid = r.astype(BF16)
    lo = (r - mid.astype(F32)).astype(BF16)
    return hi, mid, lo


def _gates_kernel(x_ref, dtb_ref, rate_ref, o_ref, *, ts):
    G = GDN_BLOCK
    row = lax.broadcasted_iota(jnp.int32, (G, G), 0)
    col = lax.broadcasted_iota(jnp.int32, (G, G), 1)
    lower = (row >= col).astype(BF16)
    upper = (row <= col).astype(BF16)
    lane = lax.broadcasted_iota(jnp.int32, (G, GDN_GATE_W), 1)
    for c in range(ts // G):
        x = x_ref[c * G:(c + 1) * G, :]
        beta = jax.nn.sigmoid(x)
        z = x + dtb_ref[...]
        softplus = jnp.maximum(z, 0.0) + jnp.log1p(jnp.exp(-jnp.abs(z)))
        g = -rate_ref[...] * softplus
        hi, mid, lo = _split3(g)
        prefix = _dot(lower, hi) + _dot(lower, mid) + _dot(lower, lo)
        suffix = _dot(upper, hi) + _dot(upper, mid) + _dot(upper, lo)
        out = jnp.where(lane < 2 * GDN_V_HEADS, beta, jnp.where(lane < 3 * GDN_V_HEADS, prefix, suffix))
        o_ref[0, :, c * G:(c + 1) * G] = out.T


def gdn_gates(gproj, a_log, dt_bias, B, S, ts):
    zeros = jnp.zeros((2 * GDN_V_HEADS,), F32)
    dtb = jnp.concatenate([zeros, dt_bias.astype(F32).reshape(-1)]).reshape(1, GDN_GATE_W)
    rate = jnp.concatenate([zeros, jnp.exp(a_log.astype(F32)).reshape(-1)]).reshape(1, GDN_GATE_W)
    nt = S // ts
    return pl.pallas_call(
        functools.partial(_gates_kernel, ts=ts),
        grid=(B, nt),
        in_specs=[pl.BlockSpec((ts, GDN_GATE_W), lambda b, i: (b * nt + i, 0)),
                  pl.BlockSpec((1, GDN_GATE_W), lambda b, i: (0, 0)),
                  pl.BlockSpec((1, GDN_GATE_W), lambda b, i: (0, 0))],
        out_specs=pl.BlockSpec((1, GDN_GATE_W, ts), lambda b, i: (b, 0, i)),
        out_shape=jax.ShapeDtypeStruct((B, GDN_GATE_W, S), F32),
        compiler_params=_cparams(("parallel", "arbitrary")),
        name="gdn_gates",
    )(gproj, dtb, rate)


def _take_rows(x, m, second):
    G = x.shape[0]
    off = m if second else 0
    return jnp.concatenate([x[g * 2 * m + off:g * 2 * m + off + m] for g in range(G // (2 * m))], axis=0)


def _put_rows(base, upd, m, second):
    G = 2 * upd.shape[0]
    parts = []
    for g in range(G // (2 * m)):
        piece = upd[g * m:(g + 1) * m]
        lo = g * 2 * m
        if base is None:
            other = jnp.zeros_like(piece)
        else:
            other = base[lo:lo + m] if second else base[lo + m:lo + 2 * m]
        parts += [other, piece] if second else [piece, other]
    return jnp.concatenate(parts, axis=0)


def _unit_triangular_inverses(a_ref, uppers, row, col, after_level):
    G = a_ref.shape[-1]

    def siblings(k):
        return ((row >> (k + 1)) == (col >> (k + 1))) & ((row >> k) != (col >> k))

    eye = (row == col).astype(F32)
    sib = siblings(0)
    tds = [eye - jnp.where(sib, a_ref[p], 0.0) for p in range(len(uppers))]
    for k in range(1, int(math.log2(G))):
        m = 1 << k
        sib = siblings(k)
        part = m >= 8
        xs = []
        for p, (upper, td) in enumerate(zip(uppers, tds)):
            a_off = jnp.where(sib, a_ref[p], 0.0)
            if part:
                a_off = _take_rows(a_off, m, not upper)
            xs.append(_dot(a_off.astype(BF16), td.astype(BF16)))
        new = []
        for upper, td, x in zip(uppers, tds, xs):
            if part:
                x_full = _put_rows(None, x, m, not upper).astype(BF16)
                td_nz = _take_rows(td, m, not upper)
                new.append(_put_rows(td, td_nz - _dot(td_nz.astype(BF16), x_full), m, not upper))
            else:
                new.append(td - _dot(td.astype(BF16), x.astype(BF16)))
        tds = new
        after_level(k)
    return tds


def _gdn_kernel(q_ref, k_ref, kt_ref, v_ref, z_ref, g_ref, nw_ref, o_ref,
                u_s, w_s, qg_s, qkd_s, kdt_s, eg_s, a_s, st_s, o_s, *, n_blocks):
    G = GDN_BLOCK
    N = n_blocks
    row = lax.broadcasted_iota(jnp.int32, (G, G), 0)
    col = lax.broadcasted_iota(jnp.int32, (G, G), 1)
    halves = (slice(0, G), slice(G, 2 * G))
    masks = ((row >= col, row > col), (row <= col, row < col))

    def block_tokens(step, d):
        c = step if d == 0 else N - 1 - step
        return pl.ds(pl.multiple_of(c * G, G), G)

    def scan_stage1(step, slot):
        out = []
        for d in range(2):
            s = st_s[d]
            lhs = jnp.concatenate([w_s[slot, d], qg_s[slot, d]], axis=0)
            r = jnp.concatenate([_dot(lhs[:, hs], s[:, hs].astype(BF16)) for hs in halves], axis=1)
            out.append((s, r))
        return out

    def scan_stage2(step, slot, completes, stage1):
        for d, (s, r) in enumerate(stage1):
            tok = block_tokens(step, d)
            v_new = (u_s[slot, d] - r[:G]).astype(BF16)
            lhs = jnp.concatenate([qkd_s[slot, d], kdt_s[slot, d]], axis=0)
            r2 = jnp.concatenate([_dot(lhs[:, hs], v_new[:, hs]) for hs in halves], axis=1)
            st_s[d] = s * eg_s[slot, d, 0:1, :] + r2[G:]
            o = r[G:] + r2[:G]
            if completes:
                o = o + o_s[tok, :]
                z = z_ref[0, tok, :].astype(F32)
                for hs in halves:
                    zh = z[:, hs]
                    o_ref[0, tok, hs] = (_rms(o[:, hs], nw_ref[...]) * (zh * jax.nn.sigmoid(zh))).astype(o_ref.dtype)
            else:
                o_s[tok, :] = o

    def group(scans, prepares, completes):
        scan_stages = []
        carry = {}
        for step, slot in scans or ():
            scan_stages.append(lambda step=step, slot=slot: carry.update(s1=scan_stage1(step, slot)))
            scan_stages.append(lambda step=step, slot=slot: scan_stage2(step, slot, completes, carry["s1"]))

        def run_scan_stage(*_):
            if scan_stages:
                scan_stages.pop(0)()

        jobs = [(step, slot, d) for step, slot in prepares for d in range(2)]
        loaded, rhs = [], []
        for step, slot, d in jobs:
            tok = block_tokens(step, d)
            qc, kc, ktc = q_ref[0, tok, :], k_ref[0, tok, :], kt_ref[0, 0, :, tok]
            loaded.append((tok, qc, kc, ktc, _dot(kc, ktc), _dot(qc, ktc)))
        run_scan_stage()
        for j, ((step, slot, d), (tok, qc, kc, ktc, kk, qk)) in enumerate(zip(jobs, loaded)):
            tri, strict = masks[d]
            vc = v_ref[0, tok, :].astype(F32)
            qf, kf, ktf = qc.astype(F32), kc.astype(F32), ktc.astype(F32)
            for hd, hs in enumerate(halves):
                beta_r = g_ref[0, hd, d:d + 1, tok]
                gc_r = g_ref[0, hd, 2 + d:3 + d, tok]
                beta_c = jnp.broadcast_to(beta_r, (G, G)).T
                gc_c = jnp.broadcast_to(gc_r, (G, G)).T
                decay = jnp.where(tri, jnp.exp(jnp.where(tri, gc_c - gc_r, 0.0)), 0.0)
                a_s[2 * j + hd] = jnp.where(strict, beta_c * kk * decay, 0.0)
                e_c = jnp.exp(gc_c)
                rhs.append(jnp.concatenate([(beta_c * vc[:, hs]).astype(BF16),
                                            (beta_c * e_c * kf).astype(BF16)], axis=1))
                qg_s[slot, d, :, hs] = (qf * e_c).astype(BF16)
                qkd_s[slot, d, :, hs] = (qk * decay).astype(BF16)
                g_last = gc_r[:, G - 1:G] if d == 0 else gc_r[:, 0:1]
                kdt_s[slot, d, :, hs] = (ktf * jnp.exp(g_last - gc_r)).astype(BF16)
                eg_s[slot, d, :, hs] = jnp.broadcast_to(jnp.exp(g_last), (8, G))
        run_scan_stage()
        uppers = [d == 1 for _, _, d in jobs for _ in halves]
        t_invs = _unit_triangular_inverses(a_s, uppers, row, col, run_scan_stage)
        for j, (step, slot, d) in enumerate(jobs):
            for hd, hs in enumerate(halves):
                uw = _dot(t_invs[2 * j + hd].astype(BF16), rhs[2 * j + hd])
                u_s[slot, d, :, hs] = uw[:, :G]
                w_s[slot, d, :, hs] = uw[:, G:].astype(BF16)
        while scan_stages:
            run_scan_stage()

    GROUP = GDN_GROUP
    st_s[...] = jnp.zeros_like(st_s)
    group(None, [(j, j) for j in range(GROUP)], False)

    def two_groups(i, carry, completes):
        n = 2 * GROUP * i
        clamp = lambda step: jnp.minimum(step, N - 1)
        first = [(n + j, j) for j in range(GROUP)]
        second = [(n + GROUP + j, GROUP + j) for j in range(GROUP)]
        after = [(clamp(n + 2 * GROUP + j), j) for j in range(GROUP)]
        group(first, second, completes)
        group(second, after, completes)
        return carry

    per_half = N // (4 * GROUP)
    lax.fori_loop(0, per_half, functools.partial(two_groups, completes=False), 0)
    lax.fori_loop(per_half, 2 * per_half, functools.partial(two_groups, completes=True), 0)


def gdn_scan(qk, kt, v, proj, gates, norm_w):
    B, S, _ = v.shape
    G = GDN_BLOCK
    nb = S // G
    assert nb % (4 * GDN_GROUP) == 0 and GDN_V_HEADS == 2 * GDN_K_HEADS and GDN_DK == G and GDN_DV == G
    zb0 = GDN_CONV_W // (2 * GDN_DV)
    narrow = lambda f: pl.BlockSpec((1, S, G), f)
    wide = lambda f: pl.BlockSpec((1, S, 2 * G), f)
    slots = lambda rows, dtype: pltpu.VMEM((2 * GDN_GROUP, 2, rows, 2 * G), dtype)
    n_problems = GDN_GROUP * 2 * 2
    return pl.pallas_call(
        functools.partial(_gdn_kernel, n_blocks=nb),
        grid=(B, GDN_K_HEADS),
        in_specs=[narrow(lambda b, h: (b, 0, h)),
                  narrow(lambda b, h: (b, 0, GDN_K_HEADS + h)),
                  pl.BlockSpec((1, 1, GDN_DK, S), lambda b, h: (b, h, 0, 0)),
                  wide(lambda b, h: (b, 0, h)),
                  wide(lambda b, h: (b, 0, zb0 + h)),
                  pl.BlockSpec((1, 2, 4, S), lambda b, h: (b, h, 0, 0)),
                  pl.BlockSpec((1, GDN_DV), lambda b, h: (0, 0))],
        out_specs=wide(lambda b, h: (b, 0, h)),
        out_shape=jax.ShapeDtypeStruct((B, S, GDN_V_W), BF16),
        scratch_shapes=[slots(G, F32),
                        slots(G, BF16),
                        slots(G, BF16),
                        slots(G, BF16),
                        slots(G, BF16),
                        slots(8, F32),
                        pltpu.VMEM((n_problems, G, G), F32),
                        pltpu.VMEM((2, G, 2 * G), F32),
                        pltpu.VMEM((S, 2 * G), F32)],
        compiler_params=_cparams(("parallel", "arbitrary")),
        name="gdn_scan",
    )(qk, qk, kt, v, proj, gates, norm_w.reshape(1, GDN_DV).astype(F32))


def _tile(total, want):
    t = min(want, total)
    while total % t:
        t //= 2
    return t


def _encoder(x, p):
    B, S, D = x.shape
    T = B * S
    h = x.reshape(T, D)
    tm = _tile(T, 1024)

    proj = norm_matmul(h, p["norm1_w"][0], p["ev_w_in"], BF16, tm, 1024)
    mix = retention_gmlp(proj.reshape(B, S, EVEN_IN_W), p["ev_ret_decay_logit"], p["ev_ret_gn_w"],
                         p["ev_gm_ln_w"], p["ev_gm_ln_b"], p["ev_gm_ws"], p["ev_gm_bs"])
    h = matmul_residual(mix.reshape(T, EVEN_MIX_W), p["ev_w_out"], h, tm, 512)
    h = ffn(h, p["norm2_w"][0], p["ffn_w_gate"][0], p["ffn_w_up"][0], p["ffn_w_down"][0],
            p["final_norm_w"], False, tm, 256)

    proj = norm_matmul(h, p["norm1_w"][1], p["od_w_main"], BF16, tm, 1024).reshape(B, S, GDN_MAIN_W)
    gproj = norm_matmul(h, p["norm1_w"][1], p["od_w_gate"], F32, tm, GDN_GATE_W)
    ts = _tile(S, 512)
    qk = conv_silu(proj, p["od_conv_w"], p["od_qk_scale"], 0, 2 * GDN_QK_W, True, ts, 512)
    v = conv_silu(proj, p["od_conv_w"], p["od_qk_scale"], 2 * GDN_QK_W, GDN_V_W, False, ts, 512)
    gates = gdn_gates(gproj, p["od_a_log"], p["od_dt_bias"], B, S, ts)
    gates = gates.reshape(B, 4, GDN_V_HEADS, S).transpose(0, 2, 1, 3)
    kt = qk[:, :, GDN_QK_W:].reshape(B, S, GDN_K_HEADS, GDN_DK).transpose(0, 2, 3, 1)
    o = gdn_scan(qk, kt, v, proj, gates, p["od_norm_w"])
    h = matmul_residual(o.reshape(T, GDN_V_W), p["od_w_out"], h, tm, 512)
    h = ffn(h, p["norm2_w"][1], p["ffn_w_gate"][1], p["ffn_w_up"][1], p["ffn_w_down"][1],
            p["final_norm_w"], True, tm, 256)
    return h.reshape(B, S, D)


def kernel(x_prompt, x_sample, norm1_w, norm2_w, final_norm_w, ev_w_in, ev_w_out, ev_ret_decay_logit, ev_ret_gn_w, ev_gm_ln_w, ev_gm_ln_b, ev_gm_ws, ev_gm_bs, od_w_in, od_conv_w, od_a_log, od_dt_bias, od_norm_w, od_w_out, ffn_w_gate, ffn_w_up, ffn_w_down):
    qk_scale = jnp.concatenate([jnp.full((GDN_QK_W,), GDN_DK ** -0.5, F32), jnp.ones((GDN_QK_W,), F32)])
    p = {
        "norm1_w": norm1_w.astype(F32), "norm2_w": norm2_w.astype(F32), "final_norm_w": final_norm_w.astype(F32),
        "ev_w_in": ev_w_in[0].astype(BF16), "ev_w_out": ev_w_out[0].astype(BF16),
        "ev_ret_decay_logit": ev_ret_decay_logit[0], "ev_ret_gn_w": ev_ret_gn_w[0],
        "ev_gm_ln_w": ev_gm_ln_w[0], "ev_gm_ln_b": ev_gm_ln_b[0], "ev_gm_ws": ev_gm_ws[0], "ev_gm_bs": ev_gm_bs[0],
        "od_w_main": od_w_in[0, :, :GDN_MAIN_W].astype(BF16), "od_w_gate": od_w_in[0, :, GDN_MAIN_W:].astype(BF16),
        "od_conv_w": od_conv_w[0].astype(F32), "od_qk_scale": qk_scale.reshape(1, -1),
        "od_a_log": od_a_log[0], "od_dt_bias": od_dt_bias[0], "od_norm_w": od_norm_w[0],
        "od_w_out": od_w_out[0].astype(BF16),
        "ffn_w_gate": ffn_w_gate.astype(BF16), "ffn_w_up": ffn_w_up.astype(BF16), "ffn_w_down": ffn_w_down.astype(BF16),
    }
    return (_encoder(x_prompt, p), _encoder(x_sample, p))
```

```python
import functools
import math

import jax
import jax.numpy as jnp
from jax import lax
from jax.experimental import pallas as pl
from jax.experimental.pallas import tpu as pltpu

F32 = jnp.float32
BF16 = jnp.bfloat16

D_MODEL = 2048
NORM_EPS = 1e-6
L2_EPS = 1e-6
RET_HEADS = 8
RET_DK = 64
RET_DV = 128
RET_CHUNK = 128
ROPE_BASE = 10000.0
RET_QK_W = RET_HEADS * RET_DK
RET_V_W = RET_HEADS * RET_DV
GM_GROUPS = 8
GM_GROUP_DIM = 128
GM_W = GM_GROUPS * GM_GROUP_DIM
EVEN_IN_W = 2 * RET_QK_W + 2 * RET_V_W + 2 * GM_W
EVEN_MIX_W = RET_V_W + GM_W
GDN_K_HEADS = 16
GDN_V_HEADS = 32
GDN_DK = 128
GDN_DV = 128
GDN_CONV = 5
GDN_QK_W = GDN_K_HEADS * GDN_DK
GDN_V_W = GDN_V_HEADS * GDN_DV
GDN_CONV_W = 2 * GDN_QK_W + GDN_V_W
GDN_MAIN_W = GDN_CONV_W + GDN_V_W
GDN_GATE_W = 4 * GDN_V_HEADS
GDN_BLOCK = 128
GDN_GROUP = 4
LANES = 128
HALO = 16
VMEM_LIMIT = 56 * 1024 * 1024


def _cparams(sem):
    return pltpu.CompilerParams(dimension_semantics=sem, vmem_limit_bytes=VMEM_LIMIT)


def _rms(x, w):
    return x * lax.rsqrt(jnp.mean(x * x, axis=-1, keepdims=True) + NORM_EPS) * w


def _dot(a, b):
    return jnp.dot(a, b, preferred_element_type=F32)


def _norm_matmul_kernel(x_ref, nw_ref, w_ref, o_ref, hn_ref):
    @pl.when(pl.program_id(1) == 0)
    def _():
        hn_ref[...] = _rms(x_ref[...], nw_ref[...]).astype(BF16)

    o_ref[...] = _dot(hn_ref[...], w_ref[...]).astype(o_ref.dtype)


def norm_matmul(x, nw, w, out_dtype, tm, tn):
    T, D = x.shape
    N = w.shape[1]
    return pl.pallas_call(
        _norm_matmul_kernel,
        grid=(T // tm, N // tn),
        in_specs=[pl.BlockSpec((tm, D), lambda i, j: (i, 0)),
                  pl.BlockSpec((1, D), lambda i, j: (0, 0)),
                  pl.BlockSpec((D, tn), lambda i, j: (0, j))],
        out_specs=pl.BlockSpec((tm, tn), lambda i, j: (i, j)),
        out_shape=jax.ShapeDtypeStruct((T, N), out_dtype),
        scratch_shapes=[pltpu.VMEM((tm, D), BF16)],
        compiler_params=_cparams(("parallel", "arbitrary")),
        name="norm_matmul",
    )(x, nw.reshape(1, D), w)


def _matmul_residual_kernel(a_ref, w_ref, h_ref, o_ref):
    o_ref[...] = h_ref[...] + _dot(a_ref[...], w_ref[...])


def matmul_residual(a, w, h, tm, tn):
    T, K = a.shape
    N = w.shape[1]
    return pl.pallas_call(
        _matmul_residual_kernel,
        grid=(T // tm, N // tn),
        in_specs=[pl.BlockSpec((tm, K), lambda i, j: (i, 0)),
                  pl.BlockSpec((K, tn), lambda i, j: (0, j)),
                  pl.BlockSpec((tm, tn), lambda i, j: (i, j))],
        out_specs=pl.BlockSpec((tm, tn), lambda i, j: (i, j)),
        out_shape=jax.ShapeDtypeStruct((T, N), F32),
        compiler_params=_cparams(("parallel", "arbitrary")),
        name="matmul_residual",
    )(a, w, h)


def _ffn_kernel(h_ref, nw_ref, wg_ref, wu_ref, wd_ref, fw_ref, o_ref, hn_ref, *, final_norm):
    f = pl.program_id(1)

    @pl.when(f == 0)
    def _():
        h = h_ref[...]
        hn_ref[...] = _rms(h, nw_ref[...]).astype(BF16)
        o_ref[...] = h

    hn = hn_ref[...]
    gate = _dot(hn, wg_ref[...])
    up = _dot(hn, wu_ref[...])
    act = (gate * jax.nn.sigmoid(gate) * up).astype(BF16)
    o_ref[...] += _dot(act, wd_ref[...])

    if final_norm:
        @pl.when(f == pl.num_programs(1) - 1)
        def _():
            o_ref[...] = _rms(o_ref[...], fw_ref[...])


def ffn(h, nw, wg, wu, wd, fw, final_norm, tm, tf):
    T, D = h.shape
    F = wg.shape[1]
    return pl.pallas_call(
        functools.partial(_ffn_kernel, final_norm=final_norm),
        grid=(T // tm, F // tf),
        in_specs=[pl.BlockSpec((tm, D), lambda i, f: (i, 0)),
                  pl.BlockSpec((1, D), lambda i, f: (0, 0)),
                  pl.BlockSpec((D, tf), lambda i, f: (0, f)),
                  pl.BlockSpec((D, tf), lambda i, f: (0, f)),
                  pl.BlockSpec((tf, D), lambda i, f: (f, 0)),
                  pl.BlockSpec((1, D), lambda i, f: (0, 0))],
        out_specs=pl.BlockSpec((tm, D), lambda i, f: (i, 0)),
        out_shape=jax.ShapeDtypeStruct((T, D), F32),
        scratch_shapes=[pltpu.VMEM((tm, D), BF16)],
        compiler_params=_cparams(("parallel", "arbitrary")),
        name="ffn",
    )(h, nw.reshape(1, D), wg, wu, wd, fw.reshape(1, D))


def _rotary(x, cos, sin_lo, sin_hi):
    n = x.shape[-1]
    half = RET_DK // 2
    return x * cos + pltpu.roll(x, n - half, 1) * sin_lo + pltpu.roll(x, half, 1) * sin_hi


def _standardize(x):
    xc = x - jnp.mean(x, axis=-1, keepdims=True)
    return xc * lax.rsqrt(jnp.mean(xc * xc, axis=-1, keepdims=True) + NORM_EPS)


def _retgmlp_kernel(q_ref, k_ref, v_ref, g_ref, gu_ref, gv_ref,
                    cosq_ref, slq_ref, shq_ref, cosk_ref, slk_ref, shk_ref,
                    dmat_ref, dq_ref, dk_ref, cdec_ref, hmask_ref,
                    gnw_ref, lnw_ref, lnb_ref, ws_ref, bs_ref,
                    o_ref, sf_ref, sb_ref, sball_ref, *, n_chunks):
    phase = pl.program_id(1)
    n = pl.program_id(2)
    C = RET_CHUNK

    def head_k_t(k_scaled, h):
        p = h // 2
        return k_scaled[:, p * LANES:(p + 1) * LANES].T.astype(BF16)

    @pl.when(phase == 0)
    def _():
        @pl.when(n == 0)
        def _():
            sb_ref[...] = jnp.zeros_like(sb_ref)

        c = n_chunks - 1 - n
        k_r = _rotary(k_ref[0].astype(F32), cosk_ref[...], slk_ref[...], shk_ref[...])
        kb = k_r * dk_ref[1]
        sball_ref[c] = sb_ref[...].astype(BF16)
        for h in range(RET_HEADS):
            if h % 2 == 0:
                kbt = head_k_t(kb, h)
            v_h = v_ref[0, :, h * RET_DV:(h + 1) * RET_DV]
            sb_ref[h] = sb_ref[h] * cdec_ref[1, h:h + 1, :] + _dot(kbt, v_h)

    @pl.when(phase == 1)
    def _():
        @pl.when(n == 0)
        def _():
            sf_ref[...] = jnp.zeros_like(sf_ref)

        q_r = _rotary(q_ref[0].astype(F32), cosq_ref[...], slq_ref[...], shq_ref[...])
        k_r = _rotary(k_ref[0].astype(F32), cosk_ref[...], slk_ref[...], shk_ref[...])
        kf = k_r * dk_ref[0]
        qf = q_r * dq_ref[0]
        qb = q_r * dq_ref[1]
        k_bf = k_r.astype(BF16)
        for h in range(RET_HEADS):
            p = h // 2
            sl = slice(p * LANES, (p + 1) * LANES)
            m = hmask_ref[h % 2:h % 2 + 1, :]
            q_m = (q_r[:, sl] * m).astype(BF16)
            v_h = v_ref[0, :, h * RET_DV:(h + 1) * RET_DV]
            scores = lax.dot_general(q_m, k_bf[:, sl], (((1,), (1,)), ((), ())),
                                     preferred_element_type=F32) * dmat_ref[h]
            out = _dot(scores.astype(BF16), v_h)
            out += _dot((qf[:, sl] * m).astype(BF16), sf_ref[h].astype(BF16))
            out += _dot((qb[:, sl] * m).astype(BF16), sball_ref[n, h])
            if h % 2 == 0:
                kft = head_k_t(kf, h)
            sf_ref[h] = sf_ref[h] * cdec_ref[0, h:h + 1, :] + _dot(kft, v_h)
            vs = slice(h * RET_DV, (h + 1) * RET_DV)
            y = _standardize(out) * gnw_ref[:, vs]
            gate = g_ref[0, :, vs].astype(F32)
            o_ref[0, :, vs] = (gate * jax.nn.sigmoid(gate) * y).astype(o_ref.dtype)

        gu = jax.nn.gelu(gu_ref[0].astype(F32))
        gv = _standardize(jax.nn.gelu(gv_ref[0].astype(F32))) * lnw_ref[...] + lnb_ref[...]
        gv = gv.astype(BF16)
        for g in range(GM_GROUPS):
            gs = slice(g * GM_GROUP_DIM, (g + 1) * GM_GROUP_DIM)
            mixed = _dot(ws_ref[g], gv[:, gs]) + bs_ref[g]
            o_ref[0, :, RET_V_W + g * GM_GROUP_DIM:RET_V_W + (g + 1) * GM_GROUP_DIM] = (
                gu[:, gs] * mixed).astype(o_ref.dtype)


def _retention_tables(S, decay_logit):
    half = RET_DK // 2
    inv = ROPE_BASE ** (-jnp.arange(half, dtype=F32) / half)
    ang = jnp.arange(S, dtype=F32)[:, None] * inv[None, :]
    cos, sin = jnp.cos(ang), jnp.sin(ang)
    zeros = jnp.zeros_like(sin)
    reps = RET_QK_W // RET_DK
    cos_t = jnp.tile(jnp.concatenate([cos, cos], axis=1), (1, reps))
    sin_lo = jnp.tile(jnp.concatenate([-sin, zeros], axis=1), (1, reps))
    sin_hi = jnp.tile(jnp.concatenate([zeros, sin], axis=1), (1, reps))
    C = RET_CHUNK
    log_gamma = jax.nn.log_sigmoid(decay_logit.astype(F32))
    lf, lb = log_gamma[0], log_gamma[1]
    idx = jnp.arange(C, dtype=F32)
    diff = idx[:, None] - idx[None, :]
    adiff = jnp.abs(diff)[None]
    dmat = jnp.where((diff >= 0)[None], jnp.exp(adiff * lf[:, None, None]),
                     jnp.exp(adiff * lb[:, None, None]))

    def lanes(t):
        return jnp.repeat(t, RET_DK, axis=1)

    dq = jnp.stack([lanes(jnp.exp((idx + 1)[:, None] * lf[None, :])),
                    lanes(jnp.exp((C - 1 - idx)[:, None] * lb[None, :]))])
    dk = jnp.stack([lanes(jnp.exp((C - 1 - idx)[:, None] * lf[None, :])),
                    lanes(jnp.exp((idx + 1)[:, None] * lb[None, :]))])
    cdec = jnp.broadcast_to(jnp.exp(C * log_gamma)[:, :, None], (2, RET_HEADS, RET_DV))
    lane = jnp.arange(LANES)
    hmask = jnp.stack([(lane < RET_DK), (lane >= RET_DK)]).astype(F32) * (RET_DK ** -0.5)
    return cos_t, sin_lo, sin_hi, dmat, dq, dk, cdec, hmask


def retention_gmlp(proj, decay_logit, gn_w, ln_w, ln_b, ws, bs):
    B, S, _ = proj.shape
    C = RET_CHUNK
    N = S // C
    cos_t, sin_lo, sin_hi, dmat, dq, dk, cdec, hmask = _retention_tables(S, decay_logit)
    bs_b = jnp.broadcast_to(bs.astype(F32)[:, :, None], (GM_GROUPS, C, GM_GROUP_DIM))

    def kv_chunk(ph, n):
        return ph * n + (1 - ph) * (N - 1 - n)

    qw, vw = RET_QK_W, RET_V_W
    col = lambda width, start: start // width
    q_spec = pl.BlockSpec((1, C, qw), lambda b, ph, n: (b, ph * n, col(qw, 0)))
    k_spec = pl.BlockSpec((1, C, qw), lambda b, ph, n: (b, kv_chunk(ph, n), col(qw, qw)))
    v_spec = pl.BlockSpec((1, C, vw), lambda b, ph, n: (b, kv_chunk(ph, n), col(vw, 2 * qw)))
    g_spec = pl.BlockSpec((1, C, vw), lambda b, ph, n: (b, ph * n, col(vw, 2 * qw + vw)))
    gu_spec = pl.BlockSpec((1, C, vw), lambda b, ph, n: (b, ph * n, col(vw, 2 * qw + 2 * vw)))
    gv_spec = pl.BlockSpec((1, C, vw), lambda b, ph, n: (b, ph * n, col(vw, 2 * qw + 3 * vw)))
    tq_spec = pl.BlockSpec((C, qw), lambda b, ph, n: (ph * n, 0))
    tk_spec = pl.BlockSpec((C, qw), lambda b, ph, n: (kv_chunk(ph, n), 0))

    def const(shape):
        nd = len(shape)
        return pl.BlockSpec(shape, lambda b, ph, n: (0,) * nd)

    return pl.pallas_call(
        functools.partial(_retgmlp_kernel, n_chunks=N),
        grid=(B, 2, N),
        in_specs=[q_spec, k_spec, v_spec, g_spec, gu_spec, gv_spec,
                  tq_spec, tq_spec, tq_spec, tk_spec, tk_spec, tk_spec,
                  const(dmat.shape), const(dq.shape), const(dk.shape), const(cdec.shape), const(hmask.shape),
                  const((1, vw)), const((1, GM_W)), const((1, GM_W)), const(ws.shape), const(bs_b.shape)],
        out_specs=pl.BlockSpec((1, C, EVEN_MIX_W), lambda b, ph, n: (b, ph * n, 0)),
        out_shape=jax.ShapeDtypeStruct((B, S, EVEN_MIX_W), BF16),
        scratch_shapes=[pltpu.VMEM((RET_HEADS, LANES, RET_DV), F32),
                        pltpu.VMEM((RET_HEADS, LANES, RET_DV), F32),
                        pltpu.VMEM((N, RET_HEADS, LANES, RET_DV), BF16)],
        compiler_params=_cparams(("parallel", "arbitrary", "arbitrary")),
        name="retention_gmlp",
    )(proj, proj, proj, proj, proj, proj, cos_t, sin_lo, sin_hi, cos_t, sin_lo, sin_hi,
      dmat, dq, dk, cdec, hmask, gn_w.reshape(1, vw).astype(F32), ln_w.reshape(1, GM_W).astype(F32),
      ln_b.reshape(1, GM_W).astype(F32), ws.astype(BF16), bs_b)


CONV_ROWS = 128
CONV_COLS = 256


def _conv_kernel(xm_ref, xp_ref, xn_ref, cw_ref, cs_ref, sh_ref, o_ref, *, ts, l2):
    i = pl.program_id(1)
    last = pl.num_programs(1) - 1
    R = CONV_ROWS
    centre = GDN_CONV // 2
    zeros = jnp.zeros(xp_ref.shape[1:], xp_ref.dtype)
    prev = jnp.where(i > 0, xp_ref[0], zeros)
    nxt = jnp.where(i < last, xn_ref[0], zeros)
    taps = [w for w in range(GDN_CONV) if w != centre]
    for blk in range(ts // R):
        lo = blk * R
        before = prev if blk == 0 else xm_ref[0, lo - HALO:lo, :]
        after = nxt if blk == ts // R - 1 else xm_ref[0, lo + R:lo + R + HALO, :]
        mid = xm_ref[0, lo:lo + R, :]
        x_ext = jnp.concatenate([before, mid, after], axis=0)
        for c0 in range(0, x_ext.shape[1], CONV_COLS):
            cs = slice(c0, c0 + CONV_COLS)
            shifted = _dot(sh_ref[...], x_ext[:, cs])
            acc = cw_ref[centre:centre + 1, cs] * mid[:, cs].astype(F32)
            for n, w in enumerate(taps):
                acc = acc + cw_ref[w:w + 1, cs] * shifted[n * R:(n + 1) * R]
            y = acc * jax.nn.sigmoid(acc)
            if l2:
                for hd in range(CONV_COLS // GDN_DK):
                    hs = slice(c0 + hd * GDN_DK, c0 + (hd + 1) * GDN_DK)
                    yh = y[:, hd * GDN_DK:(hd + 1) * GDN_DK]
                    yh = yh * lax.rsqrt(jnp.sum(yh * yh, axis=-1, keepdims=True) + L2_EPS) * cs_ref[:, hs]
                    o_ref[0, lo:lo + R, hs] = yh.astype(o_ref.dtype)
            else:
                o_ref[0, lo:lo + R, cs] = y.astype(o_ref.dtype)


def _shift_matrix():
    R = CONV_ROWS
    centre = GDN_CONV // 2
    out_row = jnp.arange(R)[:, None]
    in_row = jnp.arange(R + 2 * HALO)[None, :]
    blocks = [(in_row == out_row + HALO + w - centre) for w in range(GDN_CONV) if w != centre]
    return jnp.concatenate(blocks, axis=0).astype(BF16)


def conv_silu(proj, conv_w, colscale, col0, width, l2, ts, tc):
    B, S, _ = proj.shape
    cb0 = col0 // tc
    nt = S // ts
    hb = ts // HALO
    sh = _shift_matrix()
    return pl.pallas_call(
        functools.partial(_conv_kernel, ts=ts, l2=l2),
        grid=(B, nt, width // tc),
        in_specs=[pl.BlockSpec((1, ts, tc), lambda b, i, j: (b, i, cb0 + j)),
                  pl.BlockSpec((1, HALO, tc), lambda b, i, j: (b, jnp.maximum(i * hb - 1, 0), cb0 + j)),
                  pl.BlockSpec((1, HALO, tc), lambda b, i, j: (b, jnp.minimum((i + 1) * hb, S // HALO - 1), cb0 + j)),
                  pl.BlockSpec((GDN_CONV, tc), lambda b, i, j: (0, cb0 + j)),
                  pl.BlockSpec((1, tc), lambda b, i, j: (0, j)),
                  pl.BlockSpec(sh.shape, lambda b, i, j: (0, 0))],
        out_specs=pl.BlockSpec((1, ts, tc), lambda b, i, j: (b, i, j)),
        out_shape=jax.ShapeDtypeStruct((B, S, width), BF16),
        compiler_params=_cparams(("parallel", "arbitrary", "arbitrary")),
        name="conv_silu_l2" if l2 else "conv_silu",
    )(proj, proj, proj, conv_w, colscale, sh)


def _split3(x):
    hi = x.astype(BF16)
    r = x - hi.astype(F32)
    mid = r.astype(BF16)
    lo = (r - mid.astype(F32)).astype(BF16)
    return hi, mid, lo


def _gates_kernel(x_ref, dtb_ref, rate_ref, o_ref, *, ts):
    G = GDN_BLOCK
    row = lax.broadcasted_iota(jnp.int32, (G, G), 0)
    col = lax.broadcasted_iota(jnp.int32, (G, G), 1)
    lower = (row >= col).astype(BF16)
    upper = (row <= col).astype(BF16)
    lane = lax.broadcasted_iota(jnp.int32, (G, GDN_GATE_W), 1)
    for c in range(ts // G):
        x = x_ref[c * G:(c + 1) * G, :]
        beta = jax.nn.sigmoid(x)
        z = x + dtb_ref[...]
        softplus = jnp.maximum(z, 0.0) + jnp.log1p(jnp.exp(-jnp.abs(z)))
        g = -rate_ref[...] * softplus
        hi, mid, lo = _split3(g)
        prefix = _dot(lower, hi) + _dot(lower, mid) + _dot(lower, lo)
        suffix = _dot(upper, hi) + _dot(upper, mid) + _dot(upper, lo)
        out = jnp.where(lane < 2 * GDN_V_HEADS, beta, jnp.where(lane < 3 * GDN_V_HEADS, prefix, suffix))
        o_ref[0, :, c * G:(c + 1) * G] = out.T


def gdn_gates(gproj, a_log, dt_bias, B, S, ts):
    zeros = jnp.zeros((2 * GDN_V_HEADS,), F32)
    dtb = jnp.concatenate([zeros, dt_bias.astype(F32).reshape(-1)]).reshape(1, GDN_GATE_W)
    rate = jnp.concatenate([zeros, jnp.exp(a_log.astype(F32)).reshape(-1)]).reshape(1, GDN_GATE_W)
    nt = S // ts
    return pl.pallas_call(
        functools.partial(_gates_kernel, ts=ts),
        grid=(B, nt),
        in_specs=[pl.BlockSpec((ts, GDN_GATE_W), lambda b, i: (b * nt + i, 0)),
                  pl.BlockSpec((1, GDN_GATE_W), lambda b, i: (0, 0)),
                  pl.BlockSpec((1, GDN_GATE_W), lambda b, i: (0, 0))],
        out_specs=pl.BlockSpec((1, GDN_GATE_W, ts), lambda b, i: (b, 0, i)),
        out_shape=jax.ShapeDtypeStruct((B, GDN_GATE_W, S), F32),
        compiler_params=_cparams(("parallel", "arbitrary")),
        name="gdn_gates",
    )(gproj, dtb, rate)


def _take_rows(x, m, second):
    G = x.shape[0]
    off = m if second else 0
    return jnp.concatenate([x[g * 2 * m + off:g * 2 * m + off + m] for g in range(G // (2 * m))], axis=0)


def _put_rows(base, upd, m, second):
    G = 2 * upd.shape[0]
    parts = []
    for g in range(G // (2 * m)):
        piece = upd[g * m:(g + 1) * m]
        lo = g * 2 * m
        if base is None:
            other = jnp.zeros_like(piece)
        else:
            other = base[lo:lo + m] if second else base[lo + m:lo + 2 * m]
        parts += [other, piece] if second else [piece, other]
    return jnp.concatenate(parts, axis=0)


def _unit_triangular_inverses(a_ref, uppers, row, col, after_level):
    G = a_ref.shape[-1]

    def siblings(k):
        return ((row >> (k + 1)) == (col >> (k + 1))) & ((row >> k) != (col >> k))

    eye = (row == col).astype(F32)
    sib = siblings(0)
    tds = [eye - jnp.where(sib, a_ref[p], 0.0) for p in range(len(uppers))]
    for k in range(1, int(math.log2(G))):
        m = 1 << k
        sib = siblings(k)
        part = m >= 8
        xs = []
        for p, (upper, td) in enumerate(zip(uppers, tds)):
            a_off = jnp.where(sib, a_ref[p], 0.0)
            if part:
                a_off = _take_rows(a_off, m, not upper)
            xs.append(_dot(a_off.astype(BF16), td.astype(BF16)))
        new = []
        for upper, td, x in zip(uppers, tds, xs):
            if part:
                x_full = _put_rows(None, x, m, not upper).astype(BF16)
                td_nz = _take_rows(td, m, not upper)
                new.append(_put_rows(td, td_nz - _dot(td_nz.astype(BF16), x_full), m, not upper))
            else:
                new.append(td - _dot(td.astype(BF16), x.astype(BF16)))
        tds = new
        after_level(k)
    return tds


def _gdn_kernel(q_ref, k_ref, kt_ref, v_ref, z_ref, g_ref, nw_ref, o_ref,
                u_s, w_s, qg_s, qkd_s, kdt_s, eg_s, a_s, st_s, o_s, *, n_blocks):
    G = GDN_BLOCK
    N = n_blocks
    row = lax.broadcasted_iota(jnp.int32, (G, G), 0)
    col = lax.broadcasted_iota(jnp.int32, (G, G), 1)
    halves = (slice(0, G), slice(G, 2 * G))
    masks = (row >= col, row <= col)

    def block_tokens(step, d):
        c = step if d == 0 else N - 1 - step
        return pl.ds(pl.multiple_of(c * G, G), G)

    def scan_stage1(step, slot):
        out = []
        for d in range(2):
            s = st_s[d]
            lhs = jnp.concatenate([w_s[slot, d], qg_s[slot, d]], axis=0)
            r = jnp.concatenate([_dot(lhs[:, hs], s[:, hs].astype(BF16)) for hs in halves], axis=1)
            out.append((s, r))
        return out

    def scan_stage2(step, slot, completes, stage1):
        for d, (s, r) in enumerate(stage1):
            tok = block_tokens(step, d)
            v_new = (u_s[slot, d] - r[:G]).astype(BF16)
            lhs = jnp.concatenate([qkd_s[slot, d], kdt_s[slot, d]], axis=0)
            r2 = jnp.concatenate([_dot(lhs[:, hs], v_new[:, hs]) for hs in halves], axis=1)
            st_s[d] = s * eg_s[slot, d, 0:1, :] + r2[G:]
            o = r[G:] + r2[:G]
            if completes:
                o = o + o_s[tok, :]
                z = z_ref[0, tok, :].astype(F32)
                for hs in halves:
                    zh = z[:, hs]
                    o_ref[0, tok, hs] = (_rms(o[:, hs], nw_ref[...]) * (zh * jax.nn.sigmoid(zh))).astype(o_ref.dtype)
            else:
                o_s[tok, :] = o

    def group(scans, prepares, completes):
        scan_stages = []
        carry = {}
        for step, slot in scans or ():
            scan_stages.append(lambda step=step, slot=slot: carry.update(s1=scan_stage1(step, slot)))
            scan_stages.append(lambda step=step, slot=slot: scan_stage2(step, slot, completes, carry["s1"]))

        def run_scan_stage(*_):
            if scan_stages:
                scan_stages.pop(0)()

        jobs = [(step, slot, d) for step, slot in prepares for d in range(2)]
        loaded, rhs, betas = [], [], []
        for step, slot, d in jobs:
            tok = block_tokens(step, d)
            qc, kc, ktc = q_ref[0, tok, :], k_ref[0, tok, :], kt_ref[0, 0, :, tok]
            loaded.append((tok, qc, kc, ktc, _dot(kc, ktc), _dot(qc, ktc)))
        run_scan_stage()
        for j, ((step, slot, d), (tok, qc, kc, ktc, kk, qk)) in enumerate(zip(jobs, loaded)):
            tri = masks[d]
            vc = v_ref[0, tok, :]
            qf, kf, ktf = qc.astype(F32), kc.astype(F32), ktc.astype(F32)
            for hd, hs in enumerate(halves):
                beta_r = g_ref[0, hd, d:d + 1, tok]
                gc_r = g_ref[0, hd, 2 + d:3 + d, tok]
                beta_c = jnp.broadcast_to(beta_r, (G, G)).T
                gc_c = jnp.broadcast_to(gc_r, (G, G)).T
                decay = jnp.where(tri, jnp.exp(jnp.where(tri, gc_c - gc_r, 0.0)), 0.0)
                a_s[2 * j + hd] = beta_c * kk * decay
                e_c = jnp.exp(gc_c)
                betas.append(beta_r)
                rhs.append(jnp.concatenate([vc[:, hs], (e_c * kf).astype(BF16)], axis=1))
                qg_s[slot, d, :, hs] = (qf * e_c).astype(BF16)
                qkd_s[slot, d, :, hs] = (qk * decay).astype(BF16)
                g_last = gc_r[:, G - 1:G] if d == 0 else gc_r[:, 0:1]
                kdt_s[slot, d, :, hs] = (ktf * jnp.exp(g_last - gc_r)).astype(BF16)
                eg_s[slot, d, :, hs] = jnp.broadcast_to(jnp.exp(g_last), (8, G))
        run_scan_stage()
        uppers = [d == 1 for _, _, d in jobs for _ in halves]
        t_invs = _unit_triangular_inverses(a_s, uppers, row, col, run_scan_stage)
        for j, (step, slot, d) in enumerate(jobs):
            for hd, hs in enumerate(halves):
                uw = _dot((t_invs[2 * j + hd] * betas[2 * j + hd]).astype(BF16), rhs[2 * j + hd])
                u_s[slot, d, :, hs] = uw[:, :G]
                w_s[slot, d, :, hs] = uw[:, G:].astype(BF16)
        while scan_stages:
            run_scan_stage()

    GROUP = GDN_GROUP
    st_s[...] = jnp.zeros_like(st_s)
    group(None, [(j, j) for j in range(GROUP)], False)

    def two_groups(i, carry, completes):
        n = 2 * GROUP * i
        clamp = lambda step: jnp.minimum(step, N - 1)
        first = [(n + j, j) for j in range(GROUP)]
        second = [(n + GROUP + j, GROUP + j) for j in range(GROUP)]
        after = [(clamp(n + 2 * GROUP + j), j) for j in range(GROUP)]
        group(first, second, completes)
        group(second, after, completes)
        return carry

    per_half = N // (4 * GROUP)
    lax.fori_loop(0, per_half, functools.partial(two_groups, completes=False), 0)
    lax.fori_loop(per_half, 2 * per_half, functools.partial(two_groups, completes=True), 0)


def gdn_scan(qk, kt, v, proj, gates, norm_w):
    B, S, _ = v.shape
    G = GDN_BLOCK
    nb = S // G
    assert nb % (4 * GDN_GROUP) == 0 and GDN_V_HEADS == 2 * GDN_K_HEADS and GDN_DK == G and GDN_DV == G
    zb0 = GDN_CONV_W // (2 * GDN_DV)
    narrow = lambda f: pl.BlockSpec((1, S, G), f)
    wide = lambda f: pl.BlockSpec((1, S, 2 * G), f)
    slots = lambda rows, dtype: pltpu.VMEM((2 * GDN_GROUP, 2, rows, 2 * G), dtype)
    n_problems = GDN_GROUP * 2 * 2
    return pl.pallas_call(
        functools.partial(_gdn_kernel, n_blocks=nb),
        grid=(B, GDN_K_HEADS),
        in_specs=[narrow(lambda b, h: (b, 0, h)),
                  narrow(lambda b, h: (b, 0, GDN_K_HEADS + h)),
                  pl.BlockSpec((1, 1, GDN_DK, S), lambda b, h: (b, h, 0, 0)),
                  wide(lambda b, h: (b, 0, h)),
                  wide(lambda b, h: (b, 0, zb0 + h)),
                  pl.BlockSpec((1, 2, 4, S), lambda b, h: (b, h, 0, 0)),
                  pl.BlockSpec((1, GDN_DV), lambda b, h: (0, 0))],
        out_specs=wide(lambda b, h: (b, 0, h)),
        out_shape=jax.ShapeDtypeStruct((B, S, GDN_V_W), BF16),
        scratch_shapes=[slots(G, F32),
                        slots(G, BF16),
                        slots(G, BF16),
                        slots(G, BF16),
                        slots(G, BF16),
                        slots(8, F32),
                        pltpu.VMEM((n_problems, G, G), F32),
                        pltpu.VMEM((2, G, 2 * G), F32),
                        pltpu.VMEM((S, 2 * G), F32)],
        compiler_params=_cparams(("parallel", "arbitrary")),
        name="gdn_scan",
    )(qk, qk, kt, v, proj, gates, norm_w.reshape(1, GDN_DV).astype(F32))


def _tile(total, want):
    t = min(want, total)
    while total % t:
        t //= 2
    return t


def _encoder(x, p):
    B, S, D = x.shape
    T = B * S
    h = x.reshape(T, D)
    tm = _tile(T, 1024)

    proj = norm_matmul(h, p["norm1_w"][0], p["ev_w_in"], BF16, tm, 1024)
    mix = retention_gmlp(proj.reshape(B, S, EVEN_IN_W), p["ev_ret_decay_logit"], p["ev_ret_gn_w"],
                         p["ev_gm_ln_w"], p["ev_gm_ln_b"], p["ev_gm_ws"], p["ev_gm_bs"])
    h = matmul_residual(mix.reshape(T, EVEN_MIX_W), p["ev_w_out"], h, tm, 512)
    h = ffn(h, p["norm2_w"][0], p["ffn_w_gate"][0], p["ffn_w_up"][0], p["ffn_w_down"][0],
            p["final_norm_w"], False, tm, 256)

    proj = norm_matmul(h, p["norm1_w"][1], p["od_w_main"], BF16, tm, 1024).reshape(B, S, GDN_MAIN_W)
    gproj = norm_matmul(h, p["norm1_w"][1], p["od_w_gate"], F32, tm, GDN_GATE_W)
    ts = _tile(S, 512)
    qk = conv_silu(proj, p["od_conv_w"], p["od_qk_scale"], 0, 2 * GDN_QK_W, True, ts, 512)
    v = conv_silu(proj, p["od_conv_w"], p["od_qk_scale"], 2 * GDN_QK_W, GDN_V_W, False, ts, 512)
    gates = gdn_gates(gproj, p["od_a_log"], p["od_dt_bias"], B, S, ts)
    gates = gates.reshape(B, 4, GDN_V_HEADS, S).transpose(0, 2, 1, 3)
    kt = qk[:, :, GDN_QK_W:].reshape(B, S, GDN_K_HEADS, GDN_DK).transpose(0, 2, 3, 1)
    o = gdn_scan(qk, kt, v, proj, gates, p["od_norm_w"])
    h = matmul_residual(o.reshape(T, GDN_V_W), p["od_w_out"], h, tm, 512)
    h = ffn(h, p["norm2_w"][1], p["ffn_w_gate"][1], p["ffn_w_up"][1], p["ffn_w_down"][1],
            p["final_norm_w"], True, tm, 256)
    return h.reshape(B, S, D)


def kernel(x_prompt, x_sample, norm1_w, norm2_w, final_norm_w, ev_w_in, ev_w_out, ev_ret_decay_logit, ev_ret_gn_w, ev_gm_ln_w, ev_gm_ln_b, ev_gm_ws, ev_gm_bs, od_w_in, od_conv_w, od_a_log, od_dt_bias, od_norm_w, od_w_out, ffn_w_gate, ffn_w_up, ffn_w_down):
    qk_scale = jnp.concatenate([jnp.full((GDN_QK_W,), GDN_DK ** -0.5, F32), jnp.ones((GDN_QK_W,), F32)])
    p = {
        "norm1_w": norm1_w.astype(F32), "norm2_w": norm2_w.astype(F32), "final_norm_w": final_norm_w.astype(F32),
        "ev_w_in": ev_w_in[0].astype(BF16), "ev_w_out": ev_w_out[0].astype(BF16),
        "ev_ret_decay_logit": ev_ret_decay_logit[0], "ev_ret_gn_w": ev_ret_gn_w[0],
        "ev_gm_ln_w": ev_gm_ln_w[0], "ev_gm_ln_b": ev_gm_ln_b[0], "ev_gm_ws": ev_gm_ws[0], "ev_gm_bs": ev_gm_bs[0],
        "od_w_main": od_w_in[0, :, :GDN_MAIN_W].astype(BF16), "od_w_gate": od_w_in[0, :, GDN_MAIN_W:].astype(BF16),
        "od_conv_w": od_conv_w[0].astype(F32), "od_qk_scale": qk_scale.reshape(1, -1),
        "od_a_log": od_a_log[0], "od_dt_bias": od_dt_bias[0], "od_norm_w": od_norm_w[0],
        "od_w_out": od_w_out[0].astype(BF16),
        "ffn_w_gate": ffn_w_gate.astype(BF16), "ffn_w_up": ffn_w_up.astype(BF16), "ffn_w_down": ffn_w_down.astype(BF16),
    }
    return (_encoder(x_prompt, p), _encoder(x_sample, p))
```

```python
import functools
import math

import jax
import jax.numpy as jnp
from jax import lax
from jax.experimental import pallas as pl
from jax.experimental.pallas import tpu as pltpu

F32 = jnp.float32
BF16 = jnp.bfloat16

D_MODEL = 2048
NORM_EPS = 1e-6
L2_EPS = 1e-6
RET_HEADS = 8
RET_DK = 64
RET_DV = 128
RET_CHUNK = 128
ROPE_BASE = 10000.0
RET_QK_W = RET_HEADS * RET_DK
RET_V_W = RET_HEADS * RET_DV
GM_GROUPS = 8
GM_GROUP_DIM = 128
GM_W = GM_GROUPS * GM_GROUP_DIM
EVEN_IN_W = 2 * RET_QK_W + 2 * RET_V_W + 2 * GM_W
EVEN_MIX_W = RET_V_W + GM_W
GDN_K_HEADS = 16
GDN_V_HEADS = 32
GDN_DK = 128
GDN_DV = 128
GDN_CONV = 5
GDN_QK_W = GDN_K_HEADS * GDN_DK
GDN_V_W = GDN_V_HEADS * GDN_DV
GDN_CONV_W = 2 * GDN_QK_W + GDN_V_W
GDN_MAIN_W = GDN_CONV_W + GDN_V_W
GDN_GATE_W = 4 * GDN_V_HEADS
GDN_BLOCK = 128
GDN_GROUP = 4
LANES = 128
SUBLANES = 8
HALO = 16
VMEM_LIMIT = 56 * 1024 * 1024


def _cparams(sem):
    return pltpu.CompilerParams(dimension_semantics=sem, vmem_limit_bytes=VMEM_LIMIT)


def _rms(x, w):
    return x * lax.rsqrt(jnp.mean(x * x, axis=-1, keepdims=True) + NORM_EPS) * w


def _dot(a, b):
    return jnp.dot(a, b, preferred_element_type=F32)


def _norm_matmul_kernel(*refs, side):
    if side:
        x_ref, nw_ref, w_ref, ws_ref, o_ref, os_ref, hn_ref = refs
    else:
        x_ref, nw_ref, w_ref, o_ref, hn_ref = refs

    @pl.when(pl.program_id(1) == 0)
    def _():
        hn = _rms(x_ref[...], nw_ref[...]).astype(BF16)
        hn_ref[...] = hn
        if side:
            os_ref[...] = _dot(hn, ws_ref[...])

    o_ref[...] = _dot(hn_ref[...], w_ref[...]).astype(o_ref.dtype)


def norm_matmul(x, nw, w, out_dtype, tm, tn, w_side=None):
    T, D = x.shape
    N = w.shape[1]
    side = w_side is not None
    in_specs = [pl.BlockSpec((tm, D), lambda i, j: (i, 0)),
                pl.BlockSpec((1, D), lambda i, j: (0, 0)),
                pl.BlockSpec((D, tn), lambda i, j: (0, j))]
    out_specs = [pl.BlockSpec((tm, tn), lambda i, j: (i, j))]
    out_shape = [jax.ShapeDtypeStruct((T, N), out_dtype)]
    args = [x, nw.reshape(1, D), w]
    if side:
        ns = w_side.shape[1]
        in_specs.append(pl.BlockSpec((D, ns), lambda i, j: (0, 0)))
        out_specs.append(pl.BlockSpec((tm, ns), lambda i, j: (i, 0)))
        out_shape.append(jax.ShapeDtypeStruct((T, ns), F32))
        args.append(w_side)
    outs = pl.pallas_call(
        functools.partial(_norm_matmul_kernel, side=side),
        grid=(T // tm, N // tn),
        in_specs=in_specs,
        out_specs=out_specs,
        out_shape=out_shape,
        scratch_shapes=[pltpu.VMEM((tm, D), BF16)],
        compiler_params=_cparams(("parallel", "arbitrary")),
        name="norm_matmul",
    )(*args)
    return tuple(outs) if side else outs[0]


def _matmul_residual_kernel(a_ref, w_ref, h_ref, o_ref):
    o_ref[...] = h_ref[...] + _dot(a_ref[...], w_ref[...])


def matmul_residual(a, w, h, tm, tn):
    T, K = a.shape
    N = w.shape[1]
    return pl.pallas_call(
        _matmul_residual_kernel,
        grid=(T // tm, N // tn),
        in_specs=[pl.BlockSpec((tm, K), lambda i, j: (i, 0)),
                  pl.BlockSpec((K, tn), lambda i, j: (0, j)),
                  pl.BlockSpec((tm, tn), lambda i, j: (i, j))],
        out_specs=pl.BlockSpec((tm, tn), lambda i, j: (i, j)),
        out_shape=jax.ShapeDtypeStruct((T, N), F32),
        compiler_params=_cparams(("parallel", "arbitrary")),
        name="matmul_residual",
    )(a, w, h)


def _ffn_kernel(h_ref, nw_ref, wg_ref, wu_ref, wd_ref, fw_ref, o_ref, hn_ref, *, final_norm):
    f = pl.program_id(1)

    @pl.when(f == 0)
    def _():
        h = h_ref[...]
        hn_ref[...] = _rms(h, nw_ref[...]).astype(BF16)
        o_ref[...] = h

    hn = hn_ref[...]
    gate = _dot(hn, wg_ref[...])
    up = _dot(hn, wu_ref[...])
    act = (gate * jax.nn.sigmoid(gate) * up).astype(BF16)
    o_ref[...] += _dot(act, wd_ref[...])

    if final_norm:
        @pl.when(f == pl.num_programs(1) - 1)
        def _():
            o_ref[...] = _rms(o_ref[...], fw_ref[...])


def ffn(h, nw, wg, wu, wd, fw, final_norm, tm, tf):
    T, D = h.shape
    F = wg.shape[1]
    return pl.pallas_call(
        functools.partial(_ffn_kernel, final_norm=final_norm),
        grid=(T // tm, F // tf),
        in_specs=[pl.BlockSpec((tm, D), lambda i, f: (i, 0)),
                  pl.BlockSpec((1, D), lambda i, f: (0, 0)),
                  pl.BlockSpec((D, tf), lambda i, f: (0, f)),
                  pl.BlockSpec((D, tf), lambda i, f: (0, f)),
                  pl.BlockSpec((tf, D), lambda i, f: (f, 0)),
                  pl.BlockSpec((1, D), lambda i, f: (0, 0))],
        out_specs=pl.BlockSpec((tm, D), lambda i, f: (i, 0)),
        out_shape=jax.ShapeDtypeStruct((T, D), F32),
        scratch_shapes=[pltpu.VMEM((tm, D), BF16)],
        compiler_params=_cparams(("parallel", "arbitrary")),
        name="ffn",
    )(h, nw.reshape(1, D), wg, wu, wd, fw.reshape(1, D))


def _rotary(x, cos, sin_lo, sin_hi):
    n = x.shape[-1]
    half = RET_DK // 2
    return x * cos + pltpu.roll(x, n - half, 1) * sin_lo + pltpu.roll(x, half, 1) * sin_hi


def _standardize(x):
    xc = x - jnp.mean(x, axis=-1, keepdims=True)
    return xc * lax.rsqrt(jnp.mean(xc * xc, axis=-1, keepdims=True) + NORM_EPS)


def _retgmlp_kernel(q_ref, k_ref, v_ref, g_ref, gu_ref, gv_ref,
                    cosq_ref, slq_ref, shq_ref, cosk_ref, slk_ref, shk_ref,
                    dmat_ref, dq_ref, dk_ref, cdec_ref, hmask_ref,
                    gnw_ref, lnw_ref, lnb_ref, ws_ref, bs_ref,
                    o_ref, sf_ref, sb_ref, sball_ref, *, n_chunks):
    phase = pl.program_id(1)
    n = pl.program_id(2)
    C = RET_CHUNK

    def head_k_t(k_scaled, h):
        p = h // 2
        return k_scaled[:, p * LANES:(p + 1) * LANES].T.astype(BF16)

    @pl.when(phase == 0)
    def _():
        @pl.when(n == 0)
        def _():
            sb_ref[...] = jnp.zeros_like(sb_ref)

        c = n_chunks - 1 - n
        k_r = _rotary(k_ref[0].astype(F32), cosk_ref[...], slk_ref[...], shk_ref[...])
        kb = k_r * dk_ref[1]
        sball_ref[c] = sb_ref[...].astype(BF16)
        for h in range(RET_HEADS):
            if h % 2 == 0:
                kbt = head_k_t(kb, h)
            v_h = v_ref[0, :, h * RET_DV:(h + 1) * RET_DV]
            sb_ref[h] = sb_ref[h] * cdec_ref[1, h:h + 1, :] + _dot(kbt, v_h)

    @pl.when(phase == 1)
    def _():
        @pl.when(n == 0)
        def _():
            sf_ref[...] = jnp.zeros_like(sf_ref)

        q_r = _rotary(q_ref[0].astype(F32), cosq_ref[...], slq_ref[...], shq_ref[...])
        k_r = _rotary(k_ref[0].astype(F32), cosk_ref[...], slk_ref[...], shk_ref[...])
        kf = k_r * dk_ref[0]
        qf = q_r * dq_ref[0]
        qb = q_r * dq_ref[1]
        k_bf = k_r.astype(BF16)
        for h in range(RET_HEADS):
            p = h // 2
            sl = slice(p * LANES, (p + 1) * LANES)
            m = hmask_ref[h % 2:h % 2 + 1, :]
            q_m = (q_r[:, sl] * m).astype(BF16)
            v_h = v_ref[0, :, h * RET_DV:(h + 1) * RET_DV]
            scores = lax.dot_general(q_m, k_bf[:, sl], (((1,), (1,)), ((), ())),
                                     preferred_element_type=F32) * dmat_ref[h]
            out = _dot(scores.astype(BF16), v_h)
            out += _dot((qf[:, sl] * m).astype(BF16), sf_ref[h].astype(BF16))
            out += _dot((qb[:, sl] * m).astype(BF16), sball_ref[n, h])
            if h % 2 == 0:
                kft = head_k_t(kf, h)
            sf_ref[h] = sf_ref[h] * cdec_ref[0, h:h + 1, :] + _dot(kft, v_h)
            vs = slice(h * RET_DV, (h + 1) * RET_DV)
            y = _standardize(out) * gnw_ref[:, vs]
            gate = g_ref[0, :, vs].astype(F32)
            o_ref[0, :, vs] = (gate * jax.nn.sigmoid(gate) * y).astype(o_ref.dtype)

        gu = jax.nn.gelu(gu_ref[0].astype(F32))
        gv = _standardize(jax.nn.gelu(gv_ref[0].astype(F32))) * lnw_ref[...] + lnb_ref[...]
        gv = gv.astype(BF16)
        for g in range(GM_GROUPS):
            gs = slice(g * GM_GROUP_DIM, (g + 1) * GM_GROUP_DIM)
            mixed = _dot(ws_ref[g], gv[:, gs]) + bs_ref[g]
            o_ref[0, :, RET_V_W + g * GM_GROUP_DIM:RET_V_W + (g + 1) * GM_GROUP_DIM] = (
                gu[:, gs] * mixed).astype(o_ref.dtype)


def _retention_tables(S, decay_logit):
    half = RET_DK // 2
    inv = ROPE_BASE ** (-jnp.arange(half, dtype=F32) / half)
    ang = jnp.arange(S, dtype=F32)[:, None] * inv[None, :]
    cos, sin = jnp.cos(ang), jnp.sin(ang)
    zeros = jnp.zeros_like(sin)
    reps = RET_QK_W // RET_DK
    cos_t = jnp.tile(jnp.concatenate([cos, cos], axis=1), (1, reps))
    sin_lo = jnp.tile(jnp.concatenate([-sin, zeros], axis=1), (1, reps))
    sin_hi = jnp.tile(jnp.concatenate([zeros, sin], axis=1), (1, reps))
    C = RET_CHUNK
    log_gamma = jax.nn.log_sigmoid(decay_logit.astype(F32))
    lf, lb = log_gamma[0], log_gamma[1]
    idx = jnp.arange(C, dtype=F32)
    diff = idx[:, None] - idx[None, :]
    adiff = jnp.abs(diff)[None]
    dmat = jnp.where((diff >= 0)[None], jnp.exp(adiff * lf[:, None, None]),
                     jnp.exp(adiff * lb[:, None, None]))

    def lanes(t):
        return jnp.repeat(t, RET_DK, axis=1)

    dq = jnp.stack([lanes(jnp.exp((idx + 1)[:, None] * lf[None, :])),
                    lanes(jnp.exp((C - 1 - idx)[:, None] * lb[None, :]))])
    dk = jnp.stack([lanes(jnp.exp((C - 1 - idx)[:, None] * lf[None, :])),
                    lanes(jnp.exp((idx + 1)[:, None] * lb[None, :]))])
    cdec = jnp.broadcast_to(jnp.exp(C * log_gamma)[:, :, None], (2, RET_HEADS, RET_DV))
    lane = jnp.arange(LANES)
    hmask = jnp.stack([(lane < RET_DK), (lane >= RET_DK)]).astype(F32) * (RET_DK ** -0.5)
    return cos_t, sin_lo, sin_hi, dmat, dq, dk, cdec, hmask


def retention_gmlp(proj, decay_logit, gn_w, ln_w, ln_b, ws, bs):
    B, S, _ = proj.shape
    C = RET_CHUNK
    N = S // C
    cos_t, sin_lo, sin_hi, dmat, dq, dk, cdec, hmask = _retention_tables(S, decay_logit)
    bs_b = jnp.broadcast_to(bs.astype(F32)[:, :, None], (GM_GROUPS, C, GM_GROUP_DIM))

    def kv_chunk(ph, n):
        return ph * n + (1 - ph) * (N - 1 - n)

    qw, vw = RET_QK_W, RET_V_W
    col = lambda width, start: start // width
    q_spec = pl.BlockSpec((1, C, qw), lambda b, ph, n: (b, ph * n, col(qw, 0)))
    k_spec = pl.BlockSpec((1, C, qw), lambda b, ph, n: (b, kv_chunk(ph, n), col(qw, qw)))
    v_spec = pl.BlockSpec((1, C, vw), lambda b, ph, n: (b, kv_chunk(ph, n), col(vw, 2 * qw)))
    g_spec = pl.BlockSpec((1, C, vw), lambda b, ph, n: (b, ph * n, col(vw, 2 * qw + vw)))
    gu_spec = pl.BlockSpec((1, C, vw), lambda b, ph, n: (b, ph * n, col(vw, 2 * qw + 2 * vw)))
    gv_spec = pl.BlockSpec((1, C, vw), lambda b, ph, n: (b, ph * n, col(vw, 2 * qw + 3 * vw)))
    tq_spec = pl.BlockSpec((C, qw), lambda b, ph, n: (ph * n, 0))
    tk_spec = pl.BlockSpec((C, qw), lambda b, ph, n: (kv_chunk(ph, n), 0))

    def const(shape):
        nd = len(shape)
        return pl.BlockSpec(shape, lambda b, ph, n: (0,) * nd)

    return pl.pallas_call(
        functools.partial(_retgmlp_kernel, n_chunks=N),
        grid=(B, 2, N),
        in_specs=[q_spec, k_spec, v_spec, g_spec, gu_spec, gv_spec,
                  tq_spec, tq_spec, tq_spec, tk_spec, tk_spec, tk_spec,
                  const(dmat.shape), const(dq.shape), const(dk.shape), const(cdec.shape), const(hmask.shape),
                  const((1, vw)), const((1, GM_W)), const((1, GM_W)), const(ws.shape), const(bs_b.shape)],
        out_specs=pl.BlockSpec((1, C, EVEN_MIX_W), lambda b, ph, n: (b, ph * n, 0)),
        out_shape=jax.ShapeDtypeStruct((B, S, EVEN_MIX_W), BF16),
        scratch_shapes=[pltpu.VMEM((RET_HEADS, LANES, RET_DV), F32),
                        pltpu.VMEM((RET_HEADS, LANES, RET_DV), F32),
                        pltpu.VMEM((N, RET_HEADS, LANES, RET_DV), BF16)],
        compiler_params=_cparams(("parallel", "arbitrary", "arbitrary")),
        name="retention_gmlp",
    )(proj, proj, proj, proj, proj, proj, cos_t, sin_lo, sin_hi, cos_t, sin_lo, sin_hi,
      dmat, dq, dk, cdec, hmask, gn_w.reshape(1, vw).astype(F32), ln_w.reshape(1, GM_W).astype(F32),
      ln_b.reshape(1, GM_W).astype(F32), ws.astype(BF16), bs_b)


CONV_ROWS = 128
CONV_COLS = 256


def _conv_kernel(xm_ref, xp_ref, xn_ref, cw_ref, cs_ref, sh_ref, o_ref, *, ts, l2):
    i = pl.program_id(1)
    last = pl.num_programs(1) - 1
    R = CONV_ROWS
    centre = GDN_CONV // 2
    zeros = jnp.zeros(xp_ref.shape[1:], xp_ref.dtype)
    prev = jnp.where(i > 0, xp_ref[0], zeros)
    nxt = jnp.where(i < last, xn_ref[0], zeros)
    taps = [w for w in range(GDN_CONV) if w != centre]
    for blk in range(ts // R):
        lo = blk * R
        before = prev if blk == 0 else xm_ref[0, lo - HALO:lo, :]
        after = nxt if blk == ts // R - 1 else xm_ref[0, lo + R:lo + R + HALO, :]
        mid = xm_ref[0, lo:lo + R, :]
        x_ext = jnp.concatenate([before, mid, after], axis=0)
        for c0 in range(0, x_ext.shape[1], CONV_COLS):
            cs = slice(c0, c0 + CONV_COLS)
            shifted = _dot(sh_ref[...], x_ext[:, cs])
            acc = cw_ref[centre:centre + 1, cs] * mid[:, cs].astype(F32)
            for n, w in enumerate(taps):
                acc = acc + cw_ref[w:w + 1, cs] * shifted[n * R:(n + 1) * R]
            y = acc * jax.nn.sigmoid(acc)
            if l2:
                for hd in range(CONV_COLS // GDN_DK):
                    hs = slice(c0 + hd * GDN_DK, c0 + (hd + 1) * GDN_DK)
                    yh = y[:, hd * GDN_DK:(hd + 1) * GDN_DK]
                    yh = yh * lax.rsqrt(jnp.sum(yh * yh, axis=-1, keepdims=True) + L2_EPS) * cs_ref[:, hs]
                    o_ref[0, lo:lo + R, hs] = yh.astype(o_ref.dtype)
            else:
                o_ref[0, lo:lo + R, cs] = y.astype(o_ref.dtype)


def _shift_matrix():
    R = CONV_ROWS
    centre = GDN_CONV // 2
    out_row = jnp.arange(R)[:, None]
    in_row = jnp.arange(R + 2 * HALO)[None, :]
    blocks = [(in_row == out_row + HALO + w - centre) for w in range(GDN_CONV) if w != centre]
    return jnp.concatenate(blocks, axis=0).astype(BF16)


def conv_silu(proj, conv_w, colscale, col0, width, l2, ts, tc):
    B, S, _ = proj.shape
    cb0 = col0 // tc
    nt = S // ts
    hb = ts // HALO
    sh = _shift_matrix()
    return pl.pallas_call(
        functools.partial(_conv_kernel, ts=ts, l2=l2),
        grid=(B, nt, width // tc),
        in_specs=[pl.BlockSpec((1, ts, tc), lambda b, i, j: (b, i, cb0 + j)),
                  pl.BlockSpec((1, HALO, tc), lambda b, i, j: (b, jnp.maximum(i * hb - 1, 0), cb0 + j)),
                  pl.BlockSpec((1, HALO, tc), lambda b, i, j: (b, jnp.minimum((i + 1) * hb, S // HALO - 1), cb0 + j)),
                  pl.BlockSpec((GDN_CONV, tc), lambda b, i, j: (0, cb0 + j)),
                  pl.BlockSpec((1, tc), lambda b, i, j: (0, j)),
                  pl.BlockSpec(sh.shape, lambda b, i, j: (0, 0))],
        out_specs=pl.BlockSpec((1, ts, tc), lambda b, i, j: (b, i, j)),
        out_shape=jax.ShapeDtypeStruct((B, S, width), BF16),
        compiler_params=_cparams(("parallel", "arbitrary", "arbitrary")),
        name="conv_silu_l2" if l2 else "conv_silu",
    )(proj, proj, proj, conv_w, colscale, sh)


def _split3(x):
    hi = x.astype(BF16)
    r = x - hi.astype(F32)
    mid = r.astype(BF16)
    lo = (r - mid.astype(F32)).astype(BF16)
    return hi, mid, lo


def _gates_kernel(x_ref, dtb_ref, rate_ref, o_ref, *, ts):
    G = GDN_BLOCK
    row = lax.broadcasted_iota(jnp.int32, (G, G), 0)
    col = lax.broadcasted_iota(jnp.int32, (G, G), 1)
    lower = (row >= col).astype(BF16)
    upper = (row <= col).astype(BF16)
    lane = lax.broadcasted_iota(jnp.int32, (G, GDN_GATE_W), 1)
    for c in range(ts // G):
        x = x_ref[c * G:(c + 1) * G, :]
        beta = jax.nn.sigmoid(x)
        z = x + dtb_ref[...]
        softplus = jnp.maximum(z, 0.0) + jnp.log1p(jnp.exp(-jnp.abs(z)))
        g = -rate_ref[...] * softplus
        hi, mid, lo = _split3(g)
        prefix = _dot(lower, hi) + _dot(lower, mid) + _dot(lower, lo)
        suffix = _dot(upper, hi) + _dot(upper, mid) + _dot(upper, lo)
        out = jnp.where(lane < 2 * GDN_V_HEADS, beta, jnp.where(lane < 3 * GDN_V_HEADS, prefix, suffix))
        o_ref[0, :, c * G:(c + 1) * G] = out.T


def gdn_gates(gproj, a_log, dt_bias, B, S, ts):
    zeros = jnp.zeros((2 * GDN_V_HEADS,), F32)
    dtb = jnp.concatenate([zeros, dt_bias.astype(F32).reshape(-1)]).reshape(1, GDN_GATE_W)
    rate = jnp.concatenate([zeros, jnp.exp(a_log.astype(F32)).reshape(-1)]).reshape(1, GDN_GATE_W)
    nt = S // ts
    return pl.pallas_call(
        functools.partial(_gates_kernel, ts=ts),
        grid=(B, nt),
        in_specs=[pl.BlockSpec((ts, GDN_GATE_W), lambda b, i: (b * nt + i, 0)),
                  pl.BlockSpec((1, GDN_GATE_W), lambda b, i: (0, 0)),
                  pl.BlockSpec((1, GDN_GATE_W), lambda b, i: (0, 0))],
        out_specs=pl.BlockSpec((1, GDN_GATE_W, ts), lambda b, i: (b, 0, i)),
        out_shape=jax.ShapeDtypeStruct((B, GDN_GATE_W, S), F32),
        compiler_params=_cparams(("parallel", "arbitrary")),
        name="gdn_gates",
    )(gproj, dtb, rate)


def _take_rows(x, m, second):
    G = x.shape[0]
    off = m if second else 0
    return jnp.concatenate([x[g * 2 * m + off:g * 2 * m + off + m] for g in range(G // (2 * m))], axis=0)


def _put_rows(base, upd, m, second):
    G = 2 * upd.shape[0]
    parts = []
    for g in range(G // (2 * m)):
        piece = upd[g * m:(g + 1) * m]
        lo = g * 2 * m
        if base is None:
            other = jnp.zeros_like(piece)
        else:
            other = base[lo:lo + m] if second else base[lo + m:lo + 2 * m]
        parts += [other, piece] if second else [piece, other]
    return jnp.concatenate(parts, axis=0)


def _unit_triangular_inverses(a_ref, uppers, row, col, after_level):
    G = a_ref.shape[-1]

    def siblings(k):
        return ((row >> (k + 1)) == (col >> (k + 1))) & ((row >> k) != (col >> k))

    eye = (row == col).astype(F32)
    sib = siblings(0)
    tds = [eye - jnp.where(sib, a_ref[p], 0.0) for p in range(len(uppers))]
    for k in range(1, int(math.log2(G))):
        m = 1 << k
        sib = siblings(k)
        part = m >= SUBLANES
        xs = []
        for p, (upper, td) in enumerate(zip(uppers, tds)):
            a_off = jnp.where(sib, a_ref[p], 0.0)
            if part:
                a_off = _take_rows(a_off, m, not upper)
            xs.append(_dot(a_off.astype(BF16), td.astype(BF16)))
        new = []
        for upper, td, x in zip(uppers, tds, xs):
            if part:
                x_full = _put_rows(None, x, m, not upper).astype(BF16)
                td_nz = _take_rows(td, m, not upper)
                new.append(_put_rows(td, td_nz - _dot(td_nz.astype(BF16), x_full), m, not upper))
            else:
                new.append(td - _dot(td.astype(BF16), x.astype(BF16)))
        tds = new
        after_level(k)
    return tds


def _gdn_kernel(q_ref, k_ref, kt_ref, v_ref, z_ref, g_ref, nw_ref, o_ref,
                u_s, w_s, qg_s, qkd_s, kdt_s, eg_s, a_s, st_s, o_s, *, n_blocks):
    G = GDN_BLOCK
    N = n_blocks
    row = lax.broadcasted_iota(jnp.int32, (G, G), 0)
    col = lax.broadcasted_iota(jnp.int32, (G, G), 1)
    halves = (slice(0, G), slice(G, 2 * G))
    masks = (row >= col, row <= col)

    def block_tokens(step, d):
        c = step if d == 0 else N - 1 - step
        return pl.ds(pl.multiple_of(c * G, G), G)

    def scan_stage1(step, slot):
        out = []
        for d in range(2):
            s = st_s[d]
            lhs = jnp.concatenate([w_s[slot, d], qg_s[slot, d]], axis=0)
            r = jnp.concatenate([_dot(lhs[:, hs], s[:, hs].astype(BF16)) for hs in halves], axis=1)
            out.append((s, r))
        return out

    def scan_stage2(step, slot, completes, stage1):
        for d, (s, r) in enumerate(stage1):
            tok = block_tokens(step, d)
            v_new = (u_s[slot, d] - r[:G]).astype(BF16)
            lhs = jnp.concatenate([qkd_s[slot, d], kdt_s[slot, d]], axis=0)
            r2 = jnp.concatenate([_dot(lhs[:, hs], v_new[:, hs]) for hs in halves], axis=1)
            st_s[d] = s * eg_s[slot, d, 0:1, :] + r2[G:]
            o = r[G:] + r2[:G]
            if completes:
                o = o + o_s[tok, :]
                z = z_ref[0, tok, :].astype(F32)
                for hs in halves:
                    zh = z[:, hs]
                    o_ref[0, tok, hs] = (_rms(o[:, hs], nw_ref[...]) * (zh * jax.nn.sigmoid(zh))).astype(o_ref.dtype)
            else:
                o_s[tok, :] = o

    def group(scans, prepares, completes):
        scan_stages = []
        carry = {}
        for step, slot in scans or ():
            scan_stages.append(lambda step=step, slot=slot: carry.update(s1=scan_stage1(step, slot)))
            scan_stages.append(lambda step=step, slot=slot: scan_stage2(step, slot, completes, carry["s1"]))

        def run_scan_stage(*_):
            if scan_stages:
                scan_stages.pop(0)()

        jobs = [(step, slot, d) for step, slot in prepares for d in range(2)]
        loaded, rhs, betas = [], [], []
        for step, slot, d in jobs:
            tok = block_tokens(step, d)
            qc, kc, ktc = q_ref[0, tok, :], k_ref[0, tok, :], kt_ref[0, 0, :, tok]
            loaded.append((tok, qc, kc, ktc, _dot(kc, ktc), _dot(qc, ktc)))
        run_scan_stage()
        for j, ((step, slot, d), (tok, qc, kc, ktc, kk, qk)) in enumerate(zip(jobs, loaded)):
            tri = masks[d]
            vc = v_ref[0, tok, :]
            qf, kf, ktf = qc.astype(F32), kc.astype(F32), ktc.astype(F32)
            for hd, hs in enumerate(halves):
                beta_r = g_ref[0, hd, d:d + 1, tok]
                gc_r = g_ref[0, hd, 2 + d:3 + d, tok]
                beta_c = jnp.broadcast_to(beta_r, (G, G)).T
                gc_c = jnp.broadcast_to(gc_r, (G, G)).T
                decay = jnp.where(tri, jnp.exp(jnp.where(tri, gc_c - gc_r, 0.0)), 0.0)
                a_s[2 * j + hd] = beta_c * kk * decay
                e_c = jnp.exp(gc_c)
                betas.append(beta_r)
                rhs.append(jnp.concatenate([vc[:, hs], (e_c * kf).astype(BF16)], axis=1))
                qg_s[slot, d, :, hs] = (qf * e_c).astype(BF16)
                qkd_s[slot, d, :, hs] = (qk * decay).astype(BF16)
                g_last = gc_r[:, G - 1:G] if d == 0 else gc_r[:, 0:1]
                kdt_s[slot, d, :, hs] = (ktf * jnp.exp(g_last - gc_r)).astype(BF16)
                eg_s[slot, d, :, hs] = jnp.broadcast_to(jnp.exp(g_last), (SUBLANES, G))
        run_scan_stage()
        uppers = [d == 1 for _, _, d in jobs for _ in halves]
        t_invs = _unit_triangular_inverses(a_s, uppers, row, col, run_scan_stage)
        for j, (step, slot, d) in enumerate(jobs):
            for hd, hs in enumerate(halves):
                uw = _dot((t_invs[2 * j + hd] * betas[2 * j + hd]).astype(BF16), rhs[2 * j + hd])
                u_s[slot, d, :, hs] = uw[:, :G]
                w_s[slot, d, :, hs] = uw[:, G:].astype(BF16)
        while scan_stages:
            run_scan_stage()

    GROUP = GDN_GROUP
    st_s[...] = jnp.zeros_like(st_s)
    group(None, [(j, j) for j in range(GROUP)], False)

    def two_groups(i, carry, completes):
        n = 2 * GROUP * i
        clamp = lambda step: jnp.minimum(step, N - 1)
        first = [(n + j, j) for j in range(GROUP)]
        second = [(n + GROUP + j, GROUP + j) for j in range(GROUP)]
        after = [(clamp(n + 2 * GROUP + j), j) for j in range(GROUP)]
        group(first, second, completes)
        group(second, after, completes)
        return carry

    per_half = N // (4 * GROUP)
    lax.fori_loop(0, per_half, functools.partial(two_groups, completes=False), 0)
    lax.fori_loop(per_half, 2 * per_half, functools.partial(two_groups, completes=True), 0)


def gdn_scan(qk, kt, v, proj, gates, norm_w):
    B, S, _ = v.shape
    G = GDN_BLOCK
    nb = S // G
    assert nb % (4 * GDN_GROUP) == 0 and GDN_V_HEADS == 2 * GDN_K_HEADS and GDN_DK == G and GDN_DV == G
    zb0 = GDN_CONV_W // (2 * GDN_DV)
    narrow = lambda f: pl.BlockSpec((1, S, G), f)
    wide = lambda f: pl.BlockSpec((1, S, 2 * G), f)
    slots = lambda rows, dtype: pltpu.VMEM((2 * GDN_GROUP, 2, rows, 2 * G), dtype)
    n_problems = GDN_GROUP * 2 * 2
    return pl.pallas_call(
        functools.partial(_gdn_kernel, n_blocks=nb),
        grid=(B, GDN_K_HEADS),
        in_specs=[narrow(lambda b, h: (b, 0, h)),
                  narrow(lambda b, h: (b, 0, GDN_K_HEADS + h)),
                  pl.BlockSpec((1, 1, GDN_DK, S), lambda b, h: (b, h, 0, 0)),
                  wide(lambda b, h: (b, 0, h)),
                  wide(lambda b, h: (b, 0, zb0 + h)),
                  pl.BlockSpec((1, 2, 4, S), lambda b, h: (b, h, 0, 0)),
                  pl.BlockSpec((1, GDN_DV), lambda b, h: (0, 0))],
        out_specs=wide(lambda b, h: (b, 0, h)),
        out_shape=jax.ShapeDtypeStruct((B, S, GDN_V_W), BF16),
        scratch_shapes=[slots(G, F32),
                        slots(G, BF16),
                        slots(G, BF16),
                        slots(G, BF16),
                        slots(G, BF16),
                        slots(SUBLANES, F32),
                        pltpu.VMEM((n_problems, G, G), F32),
                        pltpu.VMEM((2, G, 2 * G), F32),
                        pltpu.VMEM((S, 2 * G), F32)],
        compiler_params=_cparams(("parallel", "arbitrary")),
        name="gdn_scan",
    )(qk, qk, kt, v, proj, gates, norm_w.reshape(1, GDN_DV).astype(F32))


def _tile(total, want):
    t = min(want, total)
    while total % t:
        t //= 2
    return t


def _encoder(x, p):
    B, S, D = x.shape
    T = B * S
    h = x.reshape(T, D)
    tm = _tile(T, 1024)

    proj = norm_matmul(h, p["norm1_w"][0], p["ev_w_in"], BF16, tm, 1024)
    mix = retention_gmlp(proj.reshape(B, S, EVEN_IN_W), p["ev_ret_decay_logit"], p["ev_ret_gn_w"],
                         p["ev_gm_ln_w"], p["ev_gm_ln_b"], p["ev_gm_ws"], p["ev_gm_bs"])
    h = matmul_residual(mix.reshape(T, EVEN_MIX_W), p["ev_w_out"], h, tm, 512)
    h = ffn(h, p["norm2_w"][0], p["ffn_w_gate"][0], p["ffn_w_up"][0], p["ffn_w_down"][0],
            p["final_norm_w"], False, tm, 256)

    proj, gproj = norm_matmul(h, p["norm1_w"][1], p["od_w_main"], BF16, tm, 1024, w_side=p["od_w_gate"])
    proj = proj.reshape(B, S, GDN_MAIN_W)
    ts = _tile(S, 1024)
    qk = conv_silu(proj, p["od_conv_w"], p["od_qk_scale"], 0, 2 * GDN_QK_W, True, ts, 512)
    v = conv_silu(proj, p["od_conv_w"], p["od_qk_scale"], 2 * GDN_QK_W, GDN_V_W, False, ts, 512)
    gates = gdn_gates(gproj, p["od_a_log"], p["od_dt_bias"], B, S, ts)
    gates = gates.reshape(B, 4, GDN_V_HEADS, S).transpose(0, 2, 1, 3)
    kt = qk[:, :, GDN_QK_W:].reshape(B, S, GDN_K_HEADS, GDN_DK).transpose(0, 2, 3, 1)
    o = gdn_scan(qk, kt, v, proj, gates, p["od_norm_w"])
    h = matmul_residual(o.reshape(T, GDN_V_W), p["od_w_out"], h, tm, 512)
    h = ffn(h, p["norm2_w"][1], p["ffn_w_gate"][1], p["ffn_w_up"][1], p["ffn_w_down"][1],
            p["final_norm_w"], True, tm, 256)
    return h.reshape(B, S, D)


def kernel(x_prompt, x_sample, norm1_w, norm2_w, final_norm_w, ev_w_in, ev_w_out, ev_ret_decay_logit, ev_ret_gn_w, ev_gm_ln_w, ev_gm_ln_b, ev_gm_ws, ev_gm_bs, od_w_in, od_conv_w, od_a_log, od_dt_bias, od_norm_w, od_w_out, ffn_w_gate, ffn_w_up, ffn_w_down):
    qk_scale = jnp.concatenate([jnp.full((GDN_QK_W,), GDN_DK ** -0.5, F32), jnp.ones((GDN_QK_W,), F32)])
    p = {
        "norm1_w": norm1_w.astype(F32), "norm2_w": norm2_w.astype(F32), "final_norm_w": final_norm_w.astype(F32),
        "ev_w_in": ev_w_in[0].astype(BF16), "ev_w_out": ev_w_out[0].astype(BF16),
        "ev_ret_decay_logit": ev_ret_decay_logit[0], "ev_ret_gn_w": ev_ret_gn_w[0],
        "ev_gm_ln_w": ev_gm_ln_w[0], "ev_gm_ln_b": ev_gm_ln_b[0], "ev_gm_ws": ev_gm_ws[0], "ev_gm_bs": ev_gm_bs[0],
        "od_w_main": od_w_in[0, :, :GDN_MAIN_W].astype(BF16), "od_w_gate": od_w_in[0, :, GDN_MAIN_W:].astype(BF16),
        "od_conv_w": od_conv_w[0].astype(F32), "od_qk_scale": qk_scale.reshape(1, -1),
        "od_a_log": od_a_log[0], "od_dt_bias": od_dt_bias[0], "od_norm_w": od_norm_w[0],
        "od_w_out": od_w_out[0].astype(BF16),
        "ffn_w_gate": ffn_w_gate.astype(BF16), "ffn_w_up": ffn_w_up.astype(BF16), "ffn_w_down": ffn_w_down.astype(BF16),
    }
    return (_encoder(x_prompt, p), _encoder(x_sample, p))
```

```python
import functools
import math

import jax
import jax.numpy as jnp
from jax import lax
from jax.experimental import pallas as pl
from jax.experimental.pallas import tpu as pltpu

F32 = jnp.float32
BF16 = jnp.bfloat16

D_MODEL = 2048
NORM_EPS = 1e-6
L2_EPS = 1e-6
RET_HEADS = 8
RET_DK = 64
RET_DV = 128
RET_CHUNK = 128
ROPE_BASE = 10000.0
RET_QK_W = RET_HEADS * RET_DK
RET_V_W = RET_HEADS * RET_DV
GM_GROUPS = 8
GM_GROUP_DIM = 128
GM_W = GM_GROUPS * GM_GROUP_DIM
EVEN_IN_W = 2 * RET_QK_W + 2 * RET_V_W + 2 * GM_W
EVEN_MIX_W = RET_V_W + GM_W
GDN_K_HEADS = 16
GDN_V_HEADS = 32
GDN_DK = 128
GDN_DV = 128
GDN_CONV = 5
GDN_QK_W = GDN_K_HEADS * GDN_DK
GDN_V_W = GDN_V_HEADS * GDN_DV
GDN_CONV_W = 2 * GDN_QK_W + GDN_V_W
GDN_MAIN_W = GDN_CONV_W + GDN_V_W
GDN_GATE_W = 4 * GDN_V_HEADS
GDN_BLOCK = 128
GDN_GROUP = 4
LANES = 128
SUBLANES = 8
HALO = 16
VMEM_LIMIT = 56 * 1024 * 1024


def _cparams(sem):
    return pltpu.CompilerParams(dimension_semantics=sem, vmem_limit_bytes=VMEM_LIMIT)


def _rms(x, w):
    return x * lax.rsqrt(jnp.mean(x * x, axis=-1, keepdims=True) + NORM_EPS) * w


def _dot(a, b):
    return jnp.dot(a, b, preferred_element_type=F32)


def _norm_matmul_kernel(*refs, side):
    if side:
        x_ref, nw_ref, w_ref, ws_ref, o_ref, os_ref, hn_ref = refs
    else:
        x_ref, nw_ref, w_ref, o_ref, hn_ref = refs

    @pl.when(pl.program_id(1) == 0)
    def _():
        hn = _rms(x_ref[...], nw_ref[...]).astype(BF16)
        hn_ref[...] = hn
        if side:
            os_ref[...] = _dot(hn, ws_ref[...])

    o_ref[...] = _dot(hn_ref[...], w_ref[...]).astype(o_ref.dtype)


def norm_matmul(x, nw, w, out_dtype, tm, tn, w_side=None):
    T, D = x.shape
    N = w.shape[1]
    side = w_side is not None
    in_specs = [pl.BlockSpec((tm, D), lambda i, j: (i, 0)),
                pl.BlockSpec((1, D), lambda i, j: (0, 0)),
                pl.BlockSpec((D, tn), lambda i, j: (0, j))]
    out_specs = [pl.BlockSpec((tm, tn), lambda i, j: (i, j))]
    out_shape = [jax.ShapeDtypeStruct((T, N), out_dtype)]
    args = [x, nw.reshape(1, D), w]
    if side:
        ns = w_side.shape[1]
        in_specs.append(pl.BlockSpec((D, ns), lambda i, j: (0, 0)))
        out_specs.append(pl.BlockSpec((tm, ns), lambda i, j: (i, 0)))
        out_shape.append(jax.ShapeDtypeStruct((T, ns), F32))
        args.append(w_side)
    outs = pl.pallas_call(
        functools.partial(_norm_matmul_kernel, side=side),
        grid=(T // tm, N // tn),
        in_specs=in_specs,
        out_specs=out_specs,
        out_shape=out_shape,
        scratch_shapes=[pltpu.VMEM((tm, D), BF16)],
        compiler_params=_cparams(("parallel", "arbitrary")),
        name="norm_matmul",
    )(*args)
    return tuple(outs) if side else outs[0]


def _matmul_residual_kernel(a_ref, w_ref, h_ref, o_ref):
    o_ref[...] = h_ref[...] + _dot(a_ref[...], w_ref[...])


def matmul_residual(a, w, h, tm, tn):
    T, K = a.shape
    N = w.shape[1]
    return pl.pallas_call(
        _matmul_residual_kernel,
        grid=(T // tm, N // tn),
        in_specs=[pl.BlockSpec((tm, K), lambda i, j: (i, 0)),
                  pl.BlockSpec((K, tn), lambda i, j: (0, j)),
                  pl.BlockSpec((tm, tn), lambda i, j: (i, j))],
        out_specs=pl.BlockSpec((tm, tn), lambda i, j: (i, j)),
        out_shape=jax.ShapeDtypeStruct((T, N), F32),
        compiler_params=_cparams(("parallel", "arbitrary")),
        name="matmul_residual",
    )(a, w, h)


def _ffn_kernel(h_ref, nw_ref, wg_ref, wu_ref, wd_ref, fw_ref, o_ref, hn_ref, *, final_norm):
    f = pl.program_id(1)

    @pl.when(f == 0)
    def _():
        h = h_ref[...]
        hn_ref[...] = _rms(h, nw_ref[...]).astype(BF16)
        o_ref[...] = h

    hn = hn_ref[...]
    gate = _dot(hn, wg_ref[...])
    up = _dot(hn, wu_ref[...])
    act = (gate * jax.nn.sigmoid(gate) * up).astype(BF16)
    o_ref[...] += _dot(act, wd_ref[...])

    if final_norm:
        @pl.when(f == pl.num_programs(1) - 1)
        def _():
            o_ref[...] = _rms(o_ref[...], fw_ref[...])


def ffn(h, nw, wg, wu, wd, fw, final_norm, tm, tf):
    T, D = h.shape
    F = wg.shape[1]
    return pl.pallas_call(
        functools.partial(_ffn_kernel, final_norm=final_norm),
        grid=(T // tm, F // tf),
        in_specs=[pl.BlockSpec((tm, D), lambda i, f: (i, 0)),
                  pl.BlockSpec((1, D), lambda i, f: (0, 0)),
                  pl.BlockSpec((D, tf), lambda i, f: (0, f)),
                  pl.BlockSpec((D, tf), lambda i, f: (0, f)),
                  pl.BlockSpec((tf, D), lambda i, f: (f, 0)),
                  pl.BlockSpec((1, D), lambda i, f: (0, 0))],
        out_specs=pl.BlockSpec((tm, D), lambda i, f: (i, 0)),
        out_shape=jax.ShapeDtypeStruct((T, D), F32),
        scratch_shapes=[pltpu.VMEM((tm, D), BF16)],
        compiler_params=_cparams(("parallel", "arbitrary")),
        name="ffn",
    )(h, nw.reshape(1, D), wg, wu, wd, fw.reshape(1, D))


def _rotary(x, cos, sin_lo, sin_hi):
    n = x.shape[-1]
    half = RET_DK // 2
    return x * cos + pltpu.roll(x, n - half, 1) * sin_lo + pltpu.roll(x, half, 1) * sin_hi


def _standardize(x):
    xc = x - jnp.mean(x, axis=-1, keepdims=True)
    return xc * lax.rsqrt(jnp.mean(xc * xc, axis=-1, keepdims=True) + NORM_EPS)


def _retgmlp_kernel(q_ref, k_ref, v_ref, g_ref, gu_ref, gv_ref,
                    cosq_ref, slq_ref, shq_ref, cosk_ref, slk_ref, shk_ref,
                    dmat_ref, dq_ref, dk_ref, cdec_ref, hmask_ref,
                    gnw_ref, lnw_ref, lnb_ref, ws_ref, bs_ref,
                    o_ref, sf_ref, sb_ref, sball_ref, *, n_chunks):
    phase = pl.program_id(1)
    n = pl.program_id(2)
    C = RET_CHUNK

    def head_k_t(k_scaled, h):
        p = h // 2
        return k_scaled[:, p * LANES:(p + 1) * LANES].T.astype(BF16)

    @pl.when(phase == 0)
    def _():
        @pl.when(n == 0)
        def _():
            sb_ref[...] = jnp.zeros_like(sb_ref)

        c = n_chunks - 1 - n
        k_r = _rotary(k_ref[0].astype(F32), cosk_ref[...], slk_ref[...], shk_ref[...])
        kb = k_r * dk_ref[1]
        sball_ref[c] = sb_ref[...].astype(BF16)
        for h in range(RET_HEADS):
            if h % 2 == 0:
                kbt = head_k_t(kb, h)
            v_h = v_ref[0, :, h * RET_DV:(h + 1) * RET_DV]
            sb_ref[h] = sb_ref[h] * cdec_ref[1, h:h + 1, :] + _dot(kbt, v_h)

    @pl.when(phase == 1)
    def _():
        @pl.when(n == 0)
        def _():
            sf_ref[...] = jnp.zeros_like(sf_ref)

        q_r = _rotary(q_ref[0].astype(F32), cosq_ref[...], slq_ref[...], shq_ref[...])
        k_r = _rotary(k_ref[0].astype(F32), cosk_ref[...], slk_ref[...], shk_ref[...])
        kf = k_r * dk_ref[0]
        qf = q_r * dq_ref[0]
        qb = q_r * dq_ref[1]
        k_bf = k_r.astype(BF16)
        for h in range(RET_HEADS):
            p = h // 2
            sl = slice(p * LANES, (p + 1) * LANES)
            m = hmask_ref[h % 2:h % 2 + 1, :]
            q_m = (q_r[:, sl] * m).astype(BF16)
            v_h = v_ref[0, :, h * RET_DV:(h + 1) * RET_DV]
            scores = lax.dot_general(q_m, k_bf[:, sl], (((1,), (1,)), ((), ())),
                                     preferred_element_type=F32) * dmat_ref[h]
            lhs = jnp.concatenate([scores.astype(BF16), (qf[:, sl] * m).astype(BF16),
                                   (qb[:, sl] * m).astype(BF16)], axis=1)
            rhs = jnp.concatenate([v_h, sf_ref[h].astype(BF16), sball_ref[n, h]], axis=0)
            out = _dot(lhs, rhs)
            if h % 2 == 0:
                kft = head_k_t(kf, h)
            sf_ref[h] = sf_ref[h] * cdec_ref[0, h:h + 1, :] + _dot(kft, v_h)
            vs = slice(h * RET_DV, (h + 1) * RET_DV)
            y = _standardize(out) * gnw_ref[:, vs]
            gate = g_ref[0, :, vs].astype(F32)
            o_ref[0, :, vs] = (gate * jax.nn.sigmoid(gate) * y).astype(o_ref.dtype)

        gu = jax.nn.gelu(gu_ref[0].astype(F32))
        gv = _standardize(jax.nn.gelu(gv_ref[0].astype(F32))) * lnw_ref[...] + lnb_ref[...]
        gv = gv.astype(BF16)
        for g in range(GM_GROUPS):
            gs = slice(g * GM_GROUP_DIM, (g + 1) * GM_GROUP_DIM)
            mixed = _dot(ws_ref[g], gv[:, gs]) + bs_ref[g]
            o_ref[0, :, RET_V_W + g * GM_GROUP_DIM:RET_V_W + (g + 1) * GM_GROUP_DIM] = (
                gu[:, gs] * mixed).astype(o_ref.dtype)


def _retention_tables(S, decay_logit):
    half = RET_DK // 2
    inv = ROPE_BASE ** (-jnp.arange(half, dtype=F32) / half)
    ang = jnp.arange(S, dtype=F32)[:, None] * inv[None, :]
    cos, sin = jnp.cos(ang), jnp.sin(ang)
    zeros = jnp.zeros_like(sin)
    reps = RET_QK_W // RET_DK
    cos_t = jnp.tile(jnp.concatenate([cos, cos], axis=1), (1, reps))
    sin_lo = jnp.tile(jnp.concatenate([-sin, zeros], axis=1), (1, reps))
    sin_hi = jnp.tile(jnp.concatenate([zeros, sin], axis=1), (1, reps))
    C = RET_CHUNK
    log_gamma = jax.nn.log_sigmoid(decay_logit.astype(F32))
    lf, lb = log_gamma[0], log_gamma[1]
    idx = jnp.arange(C, dtype=F32)
    diff = idx[:, None] - idx[None, :]
    adiff = jnp.abs(diff)[None]
    dmat = jnp.where((diff >= 0)[None], jnp.exp(adiff * lf[:, None, None]),
                     jnp.exp(adiff * lb[:, None, None]))

    def lanes(t):
        return jnp.repeat(t, RET_DK, axis=1)

    dq = jnp.stack([lanes(jnp.exp((idx + 1)[:, None] * lf[None, :])),
                    lanes(jnp.exp((C - 1 - idx)[:, None] * lb[None, :]))])
    dk = jnp.stack([lanes(jnp.exp((C - 1 - idx)[:, None] * lf[None, :])),
                    lanes(jnp.exp((idx + 1)[:, None] * lb[None, :]))])
    cdec = jnp.broadcast_to(jnp.exp(C * log_gamma)[:, :, None], (2, RET_HEADS, RET_DV))
    lane = jnp.arange(LANES)
    hmask = jnp.stack([(lane < RET_DK), (lane >= RET_DK)]).astype(F32) * (RET_DK ** -0.5)
    return cos_t, sin_lo, sin_hi, dmat, dq, dk, cdec, hmask


def retention_gmlp(proj, decay_logit, gn_w, ln_w, ln_b, ws, bs):
    B, S, _ = proj.shape
    C = RET_CHUNK
    N = S // C
    cos_t, sin_lo, sin_hi, dmat, dq, dk, cdec, hmask = _retention_tables(S, decay_logit)
    bs_b = jnp.broadcast_to(bs.astype(F32)[:, :, None], (GM_GROUPS, C, GM_GROUP_DIM))

    def kv_chunk(ph, n):
        return ph * n + (1 - ph) * (N - 1 - n)

    qw, vw = RET_QK_W, RET_V_W
    col = lambda width, start: start // width
    q_spec = pl.BlockSpec((1, C, qw), lambda b, ph, n: (b, ph * n, col(qw, 0)))
    k_spec = pl.BlockSpec((1, C, qw), lambda b, ph, n: (b, kv_chunk(ph, n), col(qw, qw)))
    v_spec = pl.BlockSpec((1, C, vw), lambda b, ph, n: (b, kv_chunk(ph, n), col(vw, 2 * qw)))
    g_spec = pl.BlockSpec((1, C, vw), lambda b, ph, n: (b, ph * n, col(vw, 2 * qw + vw)))
    gu_spec = pl.BlockSpec((1, C, vw), lambda b, ph, n: (b, ph * n, col(vw, 2 * qw + 2 * vw)))
    gv_spec = pl.BlockSpec((1, C, vw), lambda b, ph, n: (b, ph * n, col(vw, 2 * qw + 3 * vw)))
    tq_spec = pl.BlockSpec((C, qw), lambda b, ph, n: (ph * n, 0))
    tk_spec = pl.BlockSpec((C, qw), lambda b, ph, n: (kv_chunk(ph, n), 0))

    def const(shape):
        nd = len(shape)
        return pl.BlockSpec(shape, lambda b, ph, n: (0,) * nd)

    return pl.pallas_call(
        functools.partial(_retgmlp_kernel, n_chunks=N),
        grid=(B, 2, N),
        in_specs=[q_spec, k_spec, v_spec, g_spec, gu_spec, gv_spec,
                  tq_spec, tq_spec, tq_spec, tk_spec, tk_spec, tk_spec,
                  const(dmat.shape), const(dq.shape), const(dk.shape), const(cdec.shape), const(hmask.shape),
                  const((1, vw)), const((1, GM_W)), const((1, GM_W)), const(ws.shape), const(bs_b.shape)],
        out_specs=pl.BlockSpec((1, C, EVEN_MIX_W), lambda b, ph, n: (b, ph * n, 0)),
        out_shape=jax.ShapeDtypeStruct((B, S, EVEN_MIX_W), BF16),
        scratch_shapes=[pltpu.VMEM((RET_HEADS, LANES, RET_DV), F32),
                        pltpu.VMEM((RET_HEADS, LANES, RET_DV), F32),
                        pltpu.VMEM((N, RET_HEADS, LANES, RET_DV), BF16)],
        compiler_params=_cparams(("parallel", "arbitrary", "arbitrary")),
        name="retention_gmlp",
    )(proj, proj, proj, proj, proj, proj, cos_t, sin_lo, sin_hi, cos_t, sin_lo, sin_hi,
      dmat, dq, dk, cdec, hmask, gn_w.reshape(1, vw).astype(F32), ln_w.reshape(1, GM_W).astype(F32),
      ln_b.reshape(1, GM_W).astype(F32), ws.astype(BF16), bs_b)


CONV_ROWS = 128
CONV_COLS = 256


def _conv_kernel(xm_ref, xp_ref, xn_ref, cw_ref, cs_ref, sh_ref, o_ref, *, ts, l2):
    i = pl.program_id(1)
    last = pl.num_programs(1) - 1
    R = CONV_ROWS
    centre = GDN_CONV // 2
    zeros = jnp.zeros(xp_ref.shape[1:], xp_ref.dtype)
    prev = jnp.where(i > 0, xp_ref[0], zeros)
    nxt = jnp.where(i < last, xn_ref[0], zeros)
    taps = [w for w in range(GDN_CONV) if w != centre]
    for blk in range(ts // R):
        lo = blk * R
        before = prev if blk == 0 else xm_ref[0, lo - HALO:lo, :]
        after = nxt if blk == ts // R - 1 else xm_ref[0, lo + R:lo + R + HALO, :]
        mid = xm_ref[0, lo:lo + R, :]
        x_ext = jnp.concatenate([before, mid, after], axis=0)
        for c0 in range(0, x_ext.shape[1], CONV_COLS):
            cs = slice(c0, c0 + CONV_COLS)
            shifted = _dot(sh_ref[...], x_ext[:, cs])
            acc = cw_ref[centre:centre + 1, cs] * mid[:, cs].astype(F32)
            for n, w in enumerate(taps):
                acc = acc + cw_ref[w:w + 1, cs] * shifted[n * R:(n + 1) * R]
            y = acc * jax.nn.sigmoid(acc)
            if l2:
                for hd in range(CONV_COLS // GDN_DK):
                    hs = slice(c0 + hd * GDN_DK, c0 + (hd + 1) * GDN_DK)
                    yh = y[:, hd * GDN_DK:(hd + 1) * GDN_DK]
                    yh = yh * lax.rsqrt(jnp.sum(yh * yh, axis=-1, keepdims=True) + L2_EPS) * cs_ref[:, hs]
                    o_ref[0, lo:lo + R, hs] = yh.astype(o_ref.dtype)
            else:
                o_ref[0, lo:lo + R, cs] = y.astype(o_ref.dtype)


def _shift_matrix():
    R = CONV_ROWS
    centre = GDN_CONV // 2
    out_row = jnp.arange(R)[:, None]
    in_row = jnp.arange(R + 2 * HALO)[None, :]
    blocks = [(in_row == out_row + HALO + w - centre) for w in range(GDN_CONV) if w != centre]
    return jnp.concatenate(blocks, axis=0).astype(BF16)


def conv_silu(proj, conv_w, colscale, col0, width, l2, ts, tc):
    B, S, _ = proj.shape
    cb0 = col0 // tc
    nt = S // ts
    hb = ts // HALO
    sh = _shift_matrix()
    return pl.pallas_call(
        functools.partial(_conv_kernel, ts=ts, l2=l2),
        grid=(B, nt, width // tc),
        in_specs=[pl.BlockSpec((1, ts, tc), lambda b, i, j: (b, i, cb0 + j)),
                  pl.BlockSpec((1, HALO, tc), lambda b, i, j: (b, jnp.maximum(i * hb - 1, 0), cb0 + j)),
                  pl.BlockSpec((1, HALO, tc), lambda b, i, j: (b, jnp.minimum((i + 1) * hb, S // HALO - 1), cb0 + j)),
                  pl.BlockSpec((GDN_CONV, tc), lambda b, i, j: (0, cb0 + j)),
                  pl.BlockSpec((1, tc), lambda b, i, j: (0, j)),
                  pl.BlockSpec(sh.shape, lambda b, i, j: (0, 0))],
        out_specs=pl.BlockSpec((1, ts, tc), lambda b, i, j: (b, i, j)),
        out_shape=jax.ShapeDtypeStruct((B, S, width), BF16),
        compiler_params=_cparams(("parallel", "arbitrary", "arbitrary")),
        name="conv_silu_l2" if l2 else "conv_silu",
    )(proj, proj, proj, conv_w, colscale, sh)


def _split3(x):
    hi = x.astype(BF16)
    r = x - hi.astype(F32)
    mid = r.astype(BF16)
    lo = (r - mid.astype(F32)).astype(BF16)
    return hi, mid, lo


def _gates_kernel(x_ref, dtb_ref, rate_ref, o_ref, *, ts):
    G = GDN_BLOCK
    row = lax.broadcasted_iota(jnp.int32, (G, G), 0)
    col = lax.broadcasted_iota(jnp.int32, (G, G), 1)
    lower = (row >= col).astype(BF16)
    upper = (row <= col).astype(BF16)
    lane = lax.broadcasted_iota(jnp.int32, (G, GDN_GATE_W), 1)
    for c in range(ts // G):
        x = x_ref[c * G:(c + 1) * G, :]
        beta = jax.nn.sigmoid(x)
        z = x + dtb_ref[...]
        softplus = jnp.maximum(z, 0.0) + jnp.log1p(jnp.exp(-jnp.abs(z)))
        g = -rate_ref[...] * softplus
        hi, mid, lo = _split3(g)
        prefix = _dot(lower, hi) + _dot(lower, mid) + _dot(lower, lo)
        suffix = _dot(upper, hi) + _dot(upper, mid) + _dot(upper, lo)
        out = jnp.where(lane < 2 * GDN_V_HEADS, beta, jnp.where(lane < 3 * GDN_V_HEADS, prefix, suffix))
        o_ref[0, :, c * G:(c + 1) * G] = out.T


def gdn_gates(gproj, a_log, dt_bias, B, S, ts):
    zeros = jnp.zeros((2 * GDN_V_HEADS,), F32)
    dtb = jnp.concatenate([zeros, dt_bias.astype(F32).reshape(-1)]).reshape(1, GDN_GATE_W)
    rate = jnp.concatenate([zeros, jnp.exp(a_log.astype(F32)).reshape(-1)]).reshape(1, GDN_GATE_W)
    nt = S // ts
    return pl.pallas_call(
        functools.partial(_gates_kernel, ts=ts),
        grid=(B, nt),
        in_specs=[pl.BlockSpec((ts, GDN_GATE_W), lambda b, i: (b * nt + i, 0)),
                  pl.BlockSpec((1, GDN_GATE_W), lambda b, i: (0, 0)),
                  pl.BlockSpec((1, GDN_GATE_W), lambda b, i: (0, 0))],
        out_specs=pl.BlockSpec((1, GDN_GATE_W, ts), lambda b, i: (b, 0, i)),
        out_shape=jax.ShapeDtypeStruct((B, GDN_GATE_W, S), F32),
        compiler_params=_cparams(("parallel", "arbitrary")),
        name="gdn_gates",
    )(gproj, dtb, rate)


def _take_rows(x, m, second):
    G = x.shape[0]
    off = m if second else 0
    return jnp.concatenate([x[g * 2 * m + off:g * 2 * m + off + m] for g in range(G // (2 * m))], axis=0)


def _put_rows(base, upd, m, second):
    G = 2 * upd.shape[0]
    parts = []
    for g in range(G // (2 * m)):
        piece = upd[g * m:(g + 1) * m]
        lo = g * 2 * m
        if base is None:
            other = jnp.zeros_like(piece)
        else:
            other = base[lo:lo + m] if second else base[lo + m:lo + 2 * m]
        parts += [other, piece] if second else [piece, other]
    return jnp.concatenate(parts, axis=0)


def _unit_triangular_inverses(a_ref, uppers, row, col, after_level):
    G = a_ref.shape[-1]

    def siblings(k):
        return ((row >> (k + 1)) == (col >> (k + 1))) & ((row >> k) != (col >> k))

    eye = (row == col).astype(F32)
    sib = siblings(0)
    tds = [eye - jnp.where(sib, a_ref[p], 0.0) for p in range(len(uppers))]
    for k in range(1, int(math.log2(G))):
        m = 1 << k
        sib = siblings(k)
        part = m >= SUBLANES
        xs = []
        for p, (upper, td) in enumerate(zip(uppers, tds)):
            a_off = jnp.where(sib, a_ref[p], 0.0)
            if part:
                a_off = _take_rows(a_off, m, not upper)
            xs.append(_dot(a_off.astype(BF16), td.astype(BF16)))
        new = []
        for upper, td, x in zip(uppers, tds, xs):
            if part:
                x_full = _put_rows(None, x, m, not upper).astype(BF16)
                td_nz = _take_rows(td, m, not upper)
                new.append(_put_rows(td, td_nz - _dot(td_nz.astype(BF16), x_full), m, not upper))
            else:
                new.append(td - _dot(td.astype(BF16), x.astype(BF16)))
        tds = new
        after_level(k)
    return tds


def _gdn_kernel(q_ref, k_ref, kt_ref, v_ref, z_ref, g_ref, nw_ref, o_ref,
                u_s, w_s, qg_s, qkd_s, kdt_s, eg_s, a_s, st_s, o_s, *, n_blocks):
    G = GDN_BLOCK
    N = n_blocks
    row = lax.broadcasted_iota(jnp.int32, (G, G), 0)
    col = lax.broadcasted_iota(jnp.int32, (G, G), 1)
    halves = (slice(0, G), slice(G, 2 * G))
    masks = (row >= col, row <= col)

    def block_tokens(step, d):
        c = step if d == 0 else N - 1 - step
        return pl.ds(pl.multiple_of(c * G, G), G)

    def scan_stage1(step, slot):
        out = []
        for d in range(2):
            s = st_s[d]
            lhs = jnp.concatenate([w_s[slot, d], qg_s[slot, d]], axis=0)
            r = jnp.concatenate([_dot(lhs[:, hs], s[:, hs].astype(BF16)) for hs in halves], axis=1)
            out.append((s, r))
        return out

    def scan_stage2(step, slot, completes, stage1):
        for d, (s, r) in enumerate(stage1):
            tok = block_tokens(step, d)
            v_new = (u_s[slot, d] - r[:G]).astype(BF16)
            lhs = jnp.concatenate([qkd_s[slot, d], kdt_s[slot, d]], axis=0)
            r2 = jnp.concatenate([_dot(lhs[:, hs], v_new[:, hs]) for hs in halves], axis=1)
            st_s[d] = s * eg_s[slot, d, 0:1, :] + r2[G:]
            o = r[G:] + r2[:G]
            if completes:
                o = o + o_s[tok, :]
                z = z_ref[0, tok, :].astype(F32)
                for hs in halves:
                    zh = z[:, hs]
                    o_ref[0, tok, hs] = (_rms(o[:, hs], nw_ref[...]) * (zh * jax.nn.sigmoid(zh))).astype(o_ref.dtype)
            else:
                o_s[tok, :] = o

    def group(scans, prepares, completes):
        scan_stages = []
        carry = {}
        for step, slot in scans or ():
            scan_stages.append(lambda step=step, slot=slot: carry.update(s1=scan_stage1(step, slot)))
            scan_stages.append(lambda step=step, slot=slot: scan_stage2(step, slot, completes, carry["s1"]))

        def run_scan_stage(*_):
            if scan_stages:
                scan_stages.pop(0)()

        jobs = [(step, slot, d) for step, slot in prepares for d in range(2)]
        loaded, rhs, betas = [], [], []
        for step, slot, d in jobs:
            tok = block_tokens(step, d)
            qc, kc, ktc = q_ref[0, tok, :], k_ref[0, tok, :], kt_ref[0, 0, :, tok]
            loaded.append((tok, qc, kc, ktc, _dot(kc, ktc), _dot(qc, ktc)))
        run_scan_stage()
        for j, ((step, slot, d), (tok, qc, kc, ktc, kk, qk)) in enumerate(zip(jobs, loaded)):
            tri = masks[d]
            vc = v_ref[0, tok, :]
            qf, kf, ktf = qc.astype(F32), kc.astype(F32), ktc.astype(F32)
            for hd, hs in enumerate(halves):
                beta_r = g_ref[0, hd, d:d + 1, tok]
                gc_r = g_ref[0, hd, 2 + d:3 + d, tok]
                beta_c = jnp.broadcast_to(beta_r, (G, G)).T
                gc_c = jnp.broadcast_to(gc_r, (G, G)).T
                decay = jnp.where(tri, jnp.exp(jnp.where(tri, gc_c - gc_r, 0.0)), 0.0)
                a_s[2 * j + hd] = beta_c * kk * decay
                e_c = jnp.exp(gc_c)
                betas.append(beta_r)
                rhs.append(jnp.concatenate([vc[:, hs], (e_c * kf).astype(BF16)], axis=1))
                qg_s[slot, d, :, hs] = (qf * e_c).astype(BF16)
                qkd_s[slot, d, :, hs] = (qk * decay).astype(BF16)
                g_last = gc_r[:, G - 1:G] if d == 0 else gc_r[:, 0:1]
                kdt_s[slot, d, :, hs] = (ktf * jnp.exp(g_last - gc_r)).astype(BF16)
                eg_s[slot, d, :, hs] = jnp.broadcast_to(jnp.exp(g_last), (SUBLANES, G))
        run_scan_stage()
        uppers = [d == 1 for _, _, d in jobs for _ in halves]
        t_invs = _unit_triangular_inverses(a_s, uppers, row, col, run_scan_stage)
        for j, (step, slot, d) in enumerate(jobs):
            for hd, hs in enumerate(halves):
                uw = _dot((t_invs[2 * j + hd] * betas[2 * j + hd]).astype(BF16), rhs[2 * j + hd])
                u_s[slot, d, :, hs] = uw[:, :G]
                w_s[slot, d, :, hs] = uw[:, G:].astype(BF16)
        while scan_stages:
            run_scan_stage()

    GROUP = GDN_GROUP
    st_s[...] = jnp.zeros_like(st_s)
    group(None, [(j, j) for j in range(GROUP)], False)

    def two_groups(i, carry, completes):
        n = 2 * GROUP * i
        clamp = lambda step: jnp.minimum(step, N - 1)
        first = [(n + j, j) for j in range(GROUP)]
        second = [(n + GROUP + j, GROUP + j) for j in range(GROUP)]
        after = [(clamp(n + 2 * GROUP + j), j) for j in range(GROUP)]
        group(first, second, completes)
        group(second, after, completes)
        return carry

    per_half = N // (4 * GROUP)
    lax.fori_loop(0, per_half, functools.partial(two_groups, completes=False), 0)
    lax.fori_loop(per_half, 2 * per_half, functools.partial(two_groups, completes=True), 0)


def gdn_scan(qk, kt, v, proj, gates, norm_w):
    B, S, _ = v.shape
    G = GDN_BLOCK
    nb = S // G
    assert nb % (4 * GDN_GROUP) == 0 and GDN_V_HEADS == 2 * GDN_K_HEADS and GDN_DK == G and GDN_DV == G
    zb0 = GDN_CONV_W // (2 * GDN_DV)
    narrow = lambda f: pl.BlockSpec((1, S, G), f)
    wide = lambda f: pl.BlockSpec((1, S, 2 * G), f)
    slots = lambda rows, dtype: pltpu.VMEM((2 * GDN_GROUP, 2, rows, 2 * G), dtype)
    n_problems = GDN_GROUP * 2 * 2
    return pl.pallas_call(
        functools.partial(_gdn_kernel, n_blocks=nb),
        grid=(B, GDN_K_HEADS),
        in_specs=[narrow(lambda b, h: (b, 0, h)),
                  narrow(lambda b, h: (b, 0, GDN_K_HEADS + h)),
                  pl.BlockSpec((1, 1, GDN_DK, S), lambda b, h: (b, h, 0, 0)),
                  wide(lambda b, h: (b, 0, h)),
                  wide(lambda b, h: (b, 0, zb0 + h)),
                  pl.BlockSpec((1, 2, 4, S), lambda b, h: (b, h, 0, 0)),
                  pl.BlockSpec((1, GDN_DV), lambda b, h: (0, 0))],
        out_specs=wide(lambda b, h: (b, 0, h)),
        out_shape=jax.ShapeDtypeStruct((B, S, GDN_V_W), BF16),
        scratch_shapes=[slots(G, F32),
                        slots(G, BF16),
                        slots(G, BF16),
                        slots(G, BF16),
                        slots(G, BF16),
                        slots(SUBLANES, F32),
                        pltpu.VMEM((n_problems, G, G), F32),
                        pltpu.VMEM((2, G, 2 * G), F32),
                        pltpu.VMEM((S, 2 * G), F32)],
        compiler_params=_cparams(("parallel", "arbitrary")),
        name="gdn_scan",
    )(qk, qk, kt, v, proj, gates, norm_w.reshape(1, GDN_DV).astype(F32))


def _tile(total, want):
    t = min(want, total)
    while total % t:
        t //= 2
    return t


def _encoder(x, p):
    B, S, D = x.shape
    T = B * S
    h = x.reshape(T, D)
    tm = _tile(T, 1024)

    proj = norm_matmul(h, p["norm1_w"][0], p["ev_w_in"], BF16, tm, 1024)
    mix = retention_gmlp(proj.reshape(B, S, EVEN_IN_W), p["ev_ret_decay_logit"], p["ev_ret_gn_w"],
                         p["ev_gm_ln_w"], p["ev_gm_ln_b"], p["ev_gm_ws"], p["ev_gm_bs"])
    h = matmul_residual(mix.reshape(T, EVEN_MIX_W), p["ev_w_out"], h, tm, 512)
    h = ffn(h, p["norm2_w"][0], p["ffn_w_gate"][0], p["ffn_w_up"][0], p["ffn_w_down"][0],
            p["final_norm_w"], False, tm, 256)

    proj, gproj = norm_matmul(h, p["norm1_w"][1], p["od_w_main"], BF16, tm, 1024, w_side=p["od_w_gate"])
    proj = proj.reshape(B, S, GDN_MAIN_W)
    ts = _tile(S, 1024)
    qk = conv_silu(proj, p["od_conv_w"], p["od_qk_scale"], 0, 2 * GDN_QK_W, True, ts, 512)
    v = conv_silu(proj, p["od_conv_w"], p["od_qk_scale"], 2 * GDN_QK_W, GDN_V_W, False, ts, 512)
    gates = gdn_gates(gproj, p["od_a_log"], p["od_dt_bias"], B, S, ts)
    gates = gates.reshape(B, 4, GDN_V_HEADS, S).transpose(0, 2, 1, 3)
    kt = qk[:, :, GDN_QK_W:].reshape(B, S, GDN_K_HEADS, GDN_DK).transpose(0, 2, 3, 1)
    o = gdn_scan(qk, kt, v, proj, gates, p["od_norm_w"])
    h = matmul_residual(o.reshape(T, GDN_V_W), p["od_w_out"], h, tm, 512)
    h = ffn(h, p["norm2_w"][1], p["ffn_w_gate"][1], p["ffn_w_up"][1], p["ffn_w_down"][1],
            p["final_norm_w"], True, tm, 256)
    return h.reshape(B, S, D)


def kernel(x_prompt, x_sample, norm1_w, norm2_w, final_norm_w, ev_w_in, ev_w_out, ev_ret_decay_logit, ev_ret_gn_w, ev_gm_ln_w, ev_gm_ln_b, ev_gm_ws, ev_gm_bs, od_w_in, od_conv_w, od_a_log, od_dt_bias, od_norm_w, od_w_out, ffn_w_gate, ffn_w_up, ffn_w_down):
    qk_scale = jnp.concatenate([jnp.full((GDN_QK_W,), GDN_DK ** -0.5, F32), jnp.ones((GDN_QK_W,), F32)])
    p = {
        "norm1_w": norm1_w.astype(F32), "norm2_w": norm2_w.astype(F32), "final_norm_w": final_norm_w.astype(F32),
        "ev_w_in": ev_w_in[0].astype(BF16), "ev_w_out": ev_w_out[0].astype(BF16),
        "ev_ret_decay_logit": ev_ret_decay_logit[0], "ev_ret_gn_w": ev_ret_gn_w[0],
        "ev_gm_ln_w": ev_gm_ln_w[0], "ev_gm_ln_b": ev_gm_ln_b[0], "ev_gm_ws": ev_gm_ws[0], "ev_gm_bs": ev_gm_bs[0],
        "od_w_main": od_w_in[0, :, :GDN_MAIN_W].astype(BF16), "od_w_gate": od_w_in[0, :, GDN_MAIN_W:].astype(BF16),
        "od_conv_w": od_conv_w[0].astype(F32), "od_qk_scale": qk_scale.reshape(1, -1),
        "od_a_log": od_a_log[0], "od_dt_bias": od_dt_bias[0], "od_norm_w": od_norm_w[0],
        "od_w_out": od_w_out[0].astype(BF16),
        "ffn_w_gate": ffn_w_gate.astype(BF16), "ffn_w_up": ffn_w_up.astype(BF16), "ffn_w_down": ffn_w_down.astype(BF16),
    }
    return (_encoder(x_prompt, p), _encoder(x_sample, p))
```
